```python
import jax
import jax.numpy as jnp
from jax import lax
import numpy as np

D_MODEL = 1024
BATCH = 2
SEQ = 8192
DEPTH = 2

GRID_W = 64
CTX_LEN = 256
MLA_HEADS = 8
QK_NOPE = 128
QK_ROPE = 64
QK_DIM = QK_NOPE + QK_ROPE
V_HEAD = 128
Q_LORA = 256
KV_LORA = 128
ROPE_THETA = 10000.0
ATTN_SCALE = QK_DIM ** -0.5
Q_BLOCK = 128
LRU_WIDTH = D_MODEL
LRU_BLOCKS = 16
LRU_BW = LRU_WIDTH // LRU_BLOCKS
CONV_W = 4
LRU_C = 8.0
N_EXPERTS = 32
TOP_K = 4
D_FF = D_MODEL
SWIGLU_ALPHA = 1.702
SWIGLU_LIMIT = 7.0
MOE_BLOCK = 128
NORM_EPS = 1e-6
IN_SPLITS = (Q_LORA, KV_LORA, QK_ROPE, LRU_WIDTH, LRU_WIDTH, D_MODEL, D_MODEL)
D_IN = Q_LORA + KV_LORA + QK_ROPE + 2 * LRU_WIDTH + 2 * D_MODEL

kernel_name = 'hybrid_mla_rglru_moe_dit'


def rmsnorm(x, g):
    xf = x.astype(jnp.float32)
    y = xf * lax.rsqrt(jnp.mean(xf * xf, axis=-1, keepdims=True) + NORM_EPS)
    return (y * g.astype(jnp.float32)).astype(x.dtype)


def modulate(x, shift, scale):
    return x * (1 + scale) + shift


def adaln(cond, w, b):
    m = jax.nn.silu(cond) @ w + b
    return jnp.split(m[:, None, :], 6, axis=-1)


def split_cols(z):
    offs = []
    acc = 0
    for wdt in IN_SPLITS[:-1]:
        acc += wdt
        offs.append(acc)
    return jnp.split(z, offs, axis=-1)


def axial_rope_tables(n_tokens):
    rows = n_tokens // GRID_W
    row = jnp.repeat(jnp.arange(rows, dtype=jnp.float32), GRID_W)
    col = jnp.tile(jnp.arange(GRID_W, dtype=jnp.float32), rows)
    n_freq = QK_ROPE // 4
    inv = ROPE_THETA ** (-jnp.arange(n_freq, dtype=jnp.float32) / n_freq)
    ang = jnp.concatenate([row[:, None] * inv, col[:, None] * inv], axis=-1)
    return jnp.cos(ang), jnp.sin(ang)


def apply_axial_rope(x, cos, sin):
    xr = x.reshape(x.shape[:-1] + (2, 2, QK_ROPE // 4))
    x1, x2 = xr[..., 0, :], xr[..., 1, :]
    c = cos.reshape(cos.shape[:-1] + (2, QK_ROPE // 4))
    s = sin.reshape(sin.shape[:-1] + (2, QK_ROPE // 4))
    out = jnp.stack([x1 * c - x2 * s, x1 * s + x2 * c], axis=-2)
    return out.reshape(x.shape)


def mla_queries(c_q, p, cos, sin):
    b, n, _ = c_q.shape
    q = (rmsnorm(c_q, p['q_norm']) @ p['w_uq']).reshape(b, n, MLA_HEADS, QK_DIM)
    if cos is None:
        return q
    q_pe = apply_axial_rope(q[..., QK_NOPE:], cos[:, None, :], sin[:, None, :])
    return jnp.concatenate([q[..., :QK_NOPE], q_pe], axis=-1)


def mla_keys_values(c_kv, k_pe, p, cos, sin):
    b, n, _ = c_kv.shape
    kv = (rmsnorm(c_kv, p['kv_norm']) @ p['w_ukv']).reshape(b, n, MLA_HEADS, QK_NOPE + V_HEAD)
    k_nope, v = kv[..., :QK_NOPE], kv[..., QK_NOPE:]
    if cos is not None:
        k_pe = apply_axial_rope(k_pe, cos, sin)
    k_pe = jnp.broadcast_to(k_pe[:, :, None, :], (b, n, MLA_HEADS, QK_ROPE))
    return jnp.concatenate([k_nope, k_pe], axis=-1), v


def softmax_attend(q, k, v):
    s = jnp.einsum('bqhd,bkhd->bhqk', q, k).astype(jnp.float32) * ATTN_SCALE
    pr = jax.nn.softmax(s, axis=-1).astype(v.dtype)
    return jnp.einsum('bhqk,bkhd->bqhd', pr, v)


def dwconv(x, w, b):
    n = x.shape[1]
    lo = CONV_W // 2
    xp = jnp.pad(x, ((0, 0), (lo, CONV_W - 1 - lo), (0, 0)))
    y = b
    for k in range(CONV_W):
        y = y + xp[:, k:k + n] * w[k]
    return y


def linear_scan(a, u, h0, reverse):
    edge = -1 if reverse else 0
    u = u.at[:, edge].add(a[:, edge] * h0)

    def combine(left, right):
        a_l, u_l = left
        a_r, u_r = right
        return a_l * a_r, a_r * u_l + u_r

    _, h = lax.associative_scan(combine, (a, u), reverse=reverse, axis=1)
    return h


def rglru_direction(x, w_a, b_a, w_x, b_x, lam, h0, reverse):
    b, n, wdt = x.shape
    xb = x.reshape(b, n, LRU_BLOCKS, LRU_BW)
    r = jax.nn.sigmoid(jnp.einsum('bsnj,njk->bsnk', xb, w_a).reshape(b, n, wdt) + b_a)
    i = jax.nn.sigmoid(jnp.einsum('bsnj,njk->bsnk', xb, w_x).reshape(b, n, wdt) + b_x)
    log_a = -LRU_C * jax.nn.softplus(-lam.astype(jnp.float32)) * r.astype(jnp.float32)
    a = jnp.exp(log_a)
    u = jnp.sqrt(-jnp.expm1(2.0 * log_a)) * (i * x).astype(jnp.float32)
    return linear_scan(a, u, h0, reverse)


def lru_dir(p, d):
    return (p['lru_w_a'][d], p['lru_b_a'][d], p['lru_w_x'][d], p['lru_b_x'][d], p['lru_lambda'][d])


def merge_branches(att, lru, g_att, g_lru, p):
    y_att = att @ p['w_o_attn']
    y_lru = lru @ p['w_o_lru']
    m = (jax.nn.sigmoid(g_att + p['b_branch_gate'][:D_MODEL]) * y_att
         + jax.nn.sigmoid(g_lru + p['b_branch_gate'][D_MODEL:]) * y_lru)
    return m @ p['w_out']


def token_mixer(h, hc, p, cos, sin, need_ctx):
    b, n, _ = h.shape
    c_q, c_kv, k_pe, x_r, g_r, g_att, g_lru = split_cols(h @ p['w_in'])
    cc_q, cc_kv, ck_pe, cx_r, cg_r, cg_att, cg_lru = split_cols(hc @ p['w_in'])
    q = mla_queries(c_q, p, cos, sin)
    k, v = mla_keys_values(c_kv, k_pe, p, cos, sin)
    ck, cv = mla_keys_values(cc_kv, ck_pe, p, None, None)
    k_all = jnp.concatenate([ck, k], axis=1)
    v_all = jnp.concatenate([cv, v], axis=1)
    n_blk = n // Q_BLOCK
    qb = q.reshape(b, n_blk, Q_BLOCK, MLA_HEADS, QK_DIM).transpose(1, 0, 2, 3, 4)
    ob = lax.map(lambda qq: softmax_attend(qq, k_all, v_all), qb)
    att = ob.transpose(1, 0, 2, 3, 4).reshape(b, n, MLA_HEADS * V_HEAD)
    xl = dwconv(x_r, p['conv_w'], p['conv_b'])
    xc = dwconv(cx_r, p['conv_w'], p['conv_b'])
    zeros = jnp.zeros((b, LRU_WIDTH), jnp.float32)
    hc_f = rglru_direction(xc, *lru_dir(p, 0), zeros, False)
    hc_b = rglru_direction(xc, *lru_dir(p, 1), zeros, True)
    h_f = rglru_direction(xl, *lru_dir(p, 0), hc_f[:, -1], False)
    h_b = rglru_direction(xl, *lru_dir(p, 1), hc_b[:, 0], True)
    lru = (h_f + h_b).astype(h.dtype) * jax.nn.gelu(g_r)
    out = merge_branches(att, lru, g_att, g_lru, p)
    if not need_ctx:
        return out, None
    cq = mla_queries(cc_q, p, None, None)
    catt = softmax_attend(cq, ck, cv).reshape(b, -1, MLA_HEADS * V_HEAD)
    clru = (hc_f + hc_b).astype(h.dtype) * jax.nn.gelu(cg_r)
    return out, merge_branches(catt, clru, cg_att, cg_lru, p)


def clamped_swiglu(g, u):
    g = jnp.minimum(g, SWIGLU_LIMIT)
    u = jnp.clip(u, -SWIGLU_LIMIT, SWIGLU_LIMIT)
    return g * jax.nn.sigmoid(SWIGLU_ALPHA * g) * (u + 1)


def moe_ffn(h, p):
    b, n, d = h.shape
    t = b * n
    xf = h.reshape(t, d)
    logits = (xf @ p['w_router'] + p['b_router']).astype(jnp.float32)
    top_val, top_idx = lax.top_k(logits, TOP_K)
    gates = jax.nn.softmax(top_val, axis=-1)
    e_flat = top_idx.reshape(-1)
    tok_flat = jnp.repeat(jnp.arange(t, dtype=jnp.int32), TOP_K)
    w_flat = gates.reshape(-1)
    order = jnp.argsort(e_flat)
    e_sorted = e_flat[order]
    counts = jnp.bincount(e_flat, length=N_EXPERTS)
    padded = ((counts + MOE_BLOCK - 1) // MOE_BLOCK) * MOE_BLOCK
    start = jnp.cumsum(counts) - counts
    pad_end = jnp.cumsum(padded)
    pad_start = pad_end - padded
    n_assign = t * TOP_K
    dest = pad_start[e_sorted] + (jnp.arange(n_assign, dtype=jnp.int32) - start[e_sorted])
    n_blocks = (n_assign + MOE_BLOCK - 1) // MOE_BLOCK + N_EXPERTS
    n_rows = n_blocks * MOE_BLOCK
    row_tok = jnp.full((n_rows,), t, jnp.int32).at[dest].set(tok_flat[order])
    row_w = jnp.zeros((n_rows,), jnp.float32).at[dest].set(w_flat[order])
    blk_start = jnp.arange(n_blocks, dtype=jnp.int32) * MOE_BLOCK
    blk_expert = jnp.minimum(jnp.searchsorted(pad_end, blk_start, side='right'), N_EXPERTS - 1)
    x_pad = jnp.concatenate([xf, jnp.zeros((1, d), xf.dtype)], axis=0)
    rows = x_pad[row_tok].reshape(n_blocks, MOE_BLOCK, d)

    def expert_block(args):
        xb, e = args
        g = xb @ p['w_exp_gate'][e] + p['b_exp_gate'][e]
        u = xb @ p['w_exp_up'][e] + p['b_exp_up'][e]
        return clamped_swiglu(g, u) @ p['w_exp_down'][e] + p['b_exp_down'][e]

    yb = lax.map(expert_block, (rows, blk_expert)).reshape(n_rows, d)
    y = jnp.zeros((t + 1, d), h.dtype).at[row_tok].add(yb * row_w[:, None].astype(h.dtype))
    return y[:t].reshape(b, n, d)


def setup_inputs(seed: int = 0) -> dict:
    key = jax.random.key(seed)
    ks = iter(jax.random.split(key, 40))
    f32 = jnp.float32
    L, D, E = DEPTH, D_MODEL, N_EXPERTS

    def nrm(shape, scale):
        return jax.random.normal(next(ks), shape, f32) * scale

    u = jax.random.uniform(next(ks), (L, 2, LRU_WIDTH), f32, 0.9, 0.999)
    s = u ** (1.0 / LRU_C)
    lam = jnp.log(s) - jnp.log1p(-s)
    return {
        'x': nrm((BATCH, SEQ, D), 1.0),
        'c': nrm((BATCH, D), 1.0),
        'ctx': nrm((BATCH, CTX_LEN, D), 1.0),
        'c_ctx': nrm((D,), 1.0),
        'w_ada': nrm((L, D, 6 * D), 0.5 * D ** -0.5),
        'b_ada': nrm((L, 6 * D), 0.01),
        'norm_mix': 1.0 + nrm((L, D), 0.01),
        'norm_ffn': 1.0 + nrm((L, D), 0.01),
        'w_in': nrm((L, D, D_IN), D ** -0.5),
        'b_branch_gate': nrm((L, 2 * D), 0.01),
        'q_norm': 1.0 + nrm((L, Q_LORA), 0.01),
        'w_uq': nrm((L, Q_LORA, MLA_HEADS * QK_DIM), Q_LORA ** -0.5),
        'kv_norm': 1.0 + nrm((L, KV_LORA), 0.01),
        'w_ukv': nrm((L, KV_LORA, MLA_HEADS * (QK_NOPE + V_HEAD)), KV_LORA ** -0.5),
        'w_o_attn': nrm((L, MLA_HEADS * V_HEAD, D), (MLA_HEADS * V_HEAD) ** -0.5),
        'conv_w': nrm((L, CONV_W, LRU_WIDTH), CONV_W ** -0.5),
        'conv_b': nrm((L, LRU_WIDTH), 0.01),
        'lru_w_a': nrm((L, 2, LRU_BLOCKS, LRU_BW, LRU_BW), LRU_BW ** -0.5),
        'lru_b_a': nrm((L, 2, LRU_WIDTH), 0.01),
        'lru_w_x': nrm((L, 2, LRU_BLOCKS, LRU_BW, LRU_BW), LRU_BW ** -0.5),
        'lru_b_x': nrm((L, 2, LRU_WIDTH), 0.01),
        'lru_lambda': lam,
        'w_o_lru': nrm((L, LRU_WIDTH, D), LRU_WIDTH ** -0.5),
        'w_out': nrm((L, D, D), D ** -0.5),
        'w_router': nrm((L, D, E), D ** -0.5),
        'b_router': nrm((L, E), 0.01),
        'w_exp_gate': nrm((L, E, D, D_FF), D ** -0.5),
        'b_exp_gate': nrm((L, E, D_FF), 0.01),
        'w_exp_up': nrm((L, E, D, D_FF), D ** -0.5),
        'b_exp_up': nrm((L, E, D_FF), 0.01),
        'w_exp_down': nrm((L, E, D_FF, D), D_FF ** -0.5),
        'b_exp_down': nrm((L, E, D), 0.01),
        'final_norm': 1.0 + nrm((D,), 0.01),
    }


def reference(x, c, ctx, c_ctx, w_ada, b_ada, norm_mix, norm_ffn, w_in, b_branch_gate,
              q_norm, w_uq, kv_norm, w_ukv, w_o_attn, conv_w, conv_b, lru_w_a, lru_b_a,
              lru_w_x, lru_b_x, lru_lambda, w_o_lru, w_out, w_router, b_router,
              w_exp_gate, b_exp_gate, w_exp_up, b_exp_up, w_exp_down, b_exp_down, final_norm):
    n_ctx = ctx.shape[1]
    cos, sin = axial_rope_tables(x.shape[1])
    cos = cos.astype(x.dtype)
    sin = sin.astype(x.dtype)
    x_lat, x_ctx = x, ctx
    for l in range(DEPTH):
        need_ctx = l < DEPTH - 1
        p = {
            'w_in': w_in[l], 'b_branch_gate': b_branch_gate[l],
            'q_norm': q_norm[l], 'w_uq': w_uq[l], 'kv_norm': kv_norm[l], 'w_ukv': w_ukv[l],
            'w_o_attn': w_o_attn[l], 'conv_w': conv_w[l], 'conv_b': conv_b[l],
            'lru_w_a': lru_w_a[l], 'lru_b_a': lru_b_a[l], 'lru_w_x': lru_w_x[l],
            'lru_b_x': lru_b_x[l], 'lru_lambda': lru_lambda[l], 'w_o_lru': w_o_lru[l],
            'w_out': w_out[l], 'w_router': w_router[l], 'b_router': b_router[l],
            'w_exp_gate': w_exp_gate[l], 'b_exp_gate': b_exp_gate[l],
            'w_exp_up': w_exp_up[l], 'b_exp_up': b_exp_up[l],
            'w_exp_down': w_exp_down[l], 'b_exp_down': b_exp_down[l],
        }
        sh1, sc1, g1, sh2, sc2, g2 = adaln(c, w_ada[l], b_ada[l])
        csh1, csc1, cg1, csh2, csc2, cg2 = adaln(c_ctx[None, :], w_ada[l], b_ada[l])
        h = modulate(rmsnorm(x_lat, norm_mix[l]), sh1, sc1)
        hc = modulate(rmsnorm(x_ctx, norm_mix[l]), csh1, csc1)
        o, oc = token_mixer(h, hc, p, cos, sin, need_ctx)
        x_lat = x_lat + g1 * o
        h = modulate(rmsnorm(x_lat, norm_ffn[l]), sh2, sc2)
        if need_ctx:
            x_ctx = x_ctx + cg1 * oc
            hc = modulate(rmsnorm(x_ctx, norm_ffn[l]), csh2, csc2)
            y = moe_ffn(jnp.concatenate([hc, h], axis=1), p)
            x_ctx = x_ctx + cg2 * y[:, :n_ctx]
            y = y[:, n_ctx:]
        else:
            y = moe_ffn(h, p)
        x_lat = x_lat + g2 * y
    return rmsnorm(x_lat, final_norm)
```

```python
import functools
import math

import jax
import jax.numpy as jnp
from jax import lax
from jax.experimental import pallas as pl
from jax.experimental.pallas import tpu as pltpu

D = 1024
GRID_W = 64
HEADS = 8
QK_NOPE = 128
QK_ROPE = 64
QK_DIM = QK_NOPE + QK_ROPE
V_HEAD = 128
Q_LORA = 256
KV_LORA = 128
ROPE_THETA = 10000.0
ATTN_SCALE = QK_DIM ** -0.5
LRU_BLOCKS = 16
LRU_BW = D // LRU_BLOCKS
CONV_W = 4
LRU_C = 8.0
N_EXPERTS = 32
TOP_K = 4
SWIGLU_ALPHA = 1.702
SWIGLU_LIMIT = 7.0
NORM_EPS = 1e-6
D_ZA = Q_LORA + KV_LORA + 2 * QK_ROPE
D_ZB = 4 * D
QH = 256

LANES = 128
SUBLANES = 8
TM = 256
HALO = 16
TQ = 128
TK = 1024
MOE_R = 256
VMEM_LIMIT = 56 * 1024 * 1024

F32 = jnp.float32
BF16 = jnp.bfloat16
NEG_INF = float("-inf")
LOG2E = 1.4426950408889634


def _cparams(sem, vmem=None):
    return pltpu.CompilerParams(dimension_semantics=sem, vmem_limit_bytes=vmem)


def _rms(x, g):
    return x * lax.rsqrt(jnp.mean(x * x, axis=-1, keepdims=True) + NORM_EPS) * g


def _sigmoid(x):
    return 1.0 / (1.0 + jnp.exp(-x))


def _seg(i, nct):
    return jnp.where(i < nct, 0, 1)


def _ada_kernel(c_ref, w_ref, b_ref, o_ref):
    c = c_ref[...]
    s = (c * _sigmoid(c)).astype(BF16)
    o_ref[...] = jnp.dot(s, w_ref[...].astype(BF16), preferred_element_type=F32) + b_ref[...]


def _ada(cond8, w, b, tn=768):
    n = w.shape[1]
    return pl.pallas_call(
        _ada_kernel,
        out_shape=jax.ShapeDtypeStruct((SUBLANES, n), F32),
        grid=(n // tn,),
        in_specs=[pl.BlockSpec((SUBLANES, D), lambda j: (0, 0)),
                  pl.BlockSpec((D, tn), lambda j: (0, j)),
                  pl.BlockSpec((1, tn), lambda j: (0, j))],
        out_specs=pl.BlockSpec((SUBLANES, tn), lambda j: (0, j)),
        compiler_params=_cparams(("parallel",)),
        name="adaln",
    )(cond8, w, b)


def _fold_kernel(nope_ref, wuk_ref, wuv_ref, wo_ref, wqa_ref, wvo_ref):
    hp = lax.Precision.HIGHEST
    wqa_ref[...] = lax.dot_general(nope_ref[...], wuk_ref[...], (((1,), (1,)), ((), ())),
                                   precision=hp, preferred_element_type=F32)
    wvo_ref[...] = jnp.dot(wuv_ref[...], wo_ref[...], precision=hp, preferred_element_type=F32)


def _fold(nope, wuk, wuv, wo):
    sq = lambda *s: pl.BlockSpec((None,) + s, lambda h: (h, 0, 0))
    return pl.pallas_call(
        _fold_kernel,
        out_shape=(jax.ShapeDtypeStruct((HEADS, Q_LORA, KV_LORA), F32),
                   jax.ShapeDtypeStruct((HEADS, KV_LORA, D), F32)),
        grid=(HEADS,),
        in_specs=[sq(Q_LORA, QK_NOPE), sq(KV_LORA, QK_NOPE), sq(KV_LORA, V_HEAD), sq(V_HEAD, D)],
        out_specs=(sq(Q_LORA, KV_LORA), sq(KV_LORA, D)),
        compiler_params=_cparams(("parallel",)),
        name="mla_fold",
    )(nope, wuk, wuv, wo)


def _inproj_kernel(x_ref, mod_ref, g_ref, wa_ref, wb_ref, za_ref, zb_ref):
    h = _rms(x_ref[...], g_ref[...]) * (1.0 + mod_ref[1]) + mod_ref[0]
    hb = h.astype(BF16)
    za_ref[...] = jnp.dot(hb, wa_ref[...], preferred_element_type=F32)
    for j in range(D_ZB // D):
        zb_ref[:, j * D:(j + 1) * D] = jnp.dot(
            hb, wb_ref[:, j * D:(j + 1) * D], preferred_element_type=F32).astype(BF16)


def _inproj(x, mods, g, wa, wb, nct):
    b, nt, _ = x.shape
    return pl.pallas_call(
        _inproj_kernel,
        out_shape=(jax.ShapeDtypeStruct((b, nt, D_ZA), F32),
                   jax.ShapeDtypeStruct((b, nt, D_ZB), BF16)),
        grid=(b, nt // TM),
        in_specs=[pl.BlockSpec((None, TM, D), lambda bi, i: (bi, i, 0)),
                  pl.BlockSpec((None, None, 6, 1, D), lambda bi, i: (bi, _seg(i, nct), 0, 0, 0)),
                  pl.BlockSpec((1, D), lambda bi, i: (0, 0)),
                  pl.BlockSpec((D, D_ZA), lambda bi, i: (0, 0)),
                  pl.BlockSpec((D, D_ZB), lambda bi, i: (0, 0))],
        out_specs=(pl.BlockSpec((None, TM, D_ZA), lambda bi, i: (bi, i, 0)),
                   pl.BlockSpec((None, TM, D_ZB), lambda bi, i: (bi, i, 0))),
        compiler_params=_cparams(("parallel", "parallel"), VMEM_LIMIT),
        name="in_proj",
    )(x, mods, g, wa, wb)


def _mlaproj_kernel(za_ref, qn_ref, kvn_ref, wq_ref, cq_ref, sq_ref, ck_ref, q_ref, kv_ref):
    za = za_ref[...]
    cq = _rms(za[:, :Q_LORA], qn_ref[...]).astype(BF16)
    r = jnp.dot(cq, wq_ref[...], preferred_element_type=F32)
    cqt = cq_ref[...]
    sqt = sq_ref[...]
    for h in range(HEADS):
        a = r[:, h * QH:(h + 1) * QH]
        s = r[:, (HEADS + h) * QH:(HEADS + h + 1) * QH]
        q_ref[:, h * QH:(h + 1) * QH] = (a * cqt + s * sqt).astype(BF16)
    ckv = _rms(za[:, Q_LORA:Q_LORA + KV_LORA], kvn_ref[...])
    kv_ref[:, :KV_LORA] = ckv.astype(BF16)
    kv_ref[:, KV_LORA:] = (za[:, Q_LORA + KV_LORA:] * ck_ref[...]).astype(BF16)


def _mlaproj(za, qn, kvn, wq, cq, sq, ck):
    b, nt, _ = za.shape
    return pl.pallas_call(
        _mlaproj_kernel,
        out_shape=(jax.ShapeDtypeStruct((b, nt, HEADS * QH), BF16),
                   jax.ShapeDtypeStruct((b, nt, QH), BF16)),
        grid=(b, nt // TM),
        in_specs=[pl.BlockSpec((None, TM, D_ZA), lambda bi, i: (bi, i, 0)),
                  pl.BlockSpec((1, Q_LORA), lambda bi, i: (0, 0)),
                  pl.BlockSpec((1, KV_LORA), lambda bi, i: (0, 0)),
                  pl.BlockSpec((Q_LORA, 2 * HEADS * QH), lambda bi, i: (0, 0)),
                  pl.BlockSpec((TM, QH), lambda bi, i: (i, 0)),
                  pl.BlockSpec((TM, QH), lambda bi, i: (i, 0)),
                  pl.BlockSpec((TM, 2 * QK_ROPE), lambda bi, i: (i, 0))],
        out_specs=(pl.BlockSpec((None, TM, HEADS * QH), lambda bi, i: (bi, i, 0)),
                   pl.BlockSpec((None, TM, QH), lambda bi, i: (bi, i, 0))),
        compiler_params=_cparams(("parallel", "parallel"), VMEM_LIMIT),
        name="mla_proj",
    )(za, qn, kvn, wq, cq, sq, ck)


def _attn_kernel(q_ref, kv_ref, o_ref, m_s, acc_s, *, n_ctx, n_lat, n_ctx_qtiles):
    i = pl.program_id(1)
    m_s[...] = jnp.full(m_s.shape, NEG_INF, F32)
    acc_s[...] = jnp.zeros(acc_s.shape, F32)
    q = q_ref[...]

    def step(k):
        s = lax.dot_general(q, k, (((1,), (1,)), ((), ())), preferred_element_type=F32)
        m_old = m_s[...]
        m_new = jnp.maximum(m_old, jnp.max(s, axis=1, keepdims=True))
        alpha = jnp.exp2(m_old - m_new)
        p = jnp.exp2(s - m_new).astype(BF16)
        lane = lax.broadcasted_iota(jnp.int32, (k.shape[0], LANES), 1)
        ones_col = jnp.where(lane == 0, 1.0, 0.0).astype(BF16)
        v1 = jnp.concatenate([k[:, :KV_LORA], ones_col], axis=1)
        acc_s[...] = alpha * acc_s[...] + jnp.dot(p, v1, preferred_element_type=F32)
        m_s[...] = m_new

    step(kv_ref[0:n_ctx, :])

    @pl.when(i >= n_ctx_qtiles)
    def _():
        def body(t, carry):
            k0 = pl.multiple_of(n_ctx + t * TK, LANES)
            step(kv_ref[pl.ds(k0, TK), :])
            return carry
        lax.fori_loop(0, n_lat // TK, body, 0)

    acc = acc_s[...]
    o_ref[...] = (acc[:, :KV_LORA] / acc[:, KV_LORA:KV_LORA + 1]).astype(BF16)


def _attention(q, kv, n_ctx):
    b, nt, _ = kv.shape
    rows = TQ * HEADS
    kern = functools.partial(_attn_kernel, n_ctx=n_ctx, n_lat=nt - n_ctx, n_ctx_qtiles=n_ctx // TQ)
    return pl.pallas_call(
        kern,
        out_shape=jax.ShapeDtypeStruct((b, nt * HEADS, KV_LORA), BF16),
        grid=(b, nt // TQ),
        in_specs=[pl.BlockSpec((None, rows, QH), lambda bi, i: (bi, i, 0)),
                  pl.BlockSpec((None, nt, QH), lambda bi, i: (bi, 0, 0))],
        out_specs=pl.BlockSpec((None, rows, KV_LORA), lambda bi, i: (bi, i, 0)),
        scratch_shapes=[pltpu.VMEM((rows, 1), F32), pltpu.VMEM((rows, 2 * LANES), F32)],
        compiler_params=_cparams(("parallel", "arbitrary"), VMEM_LIMIT),
        name="mla_attention",
    )(q, kv)


def _lru_tile(j, nct, ntl, reverse):
    if not reverse:
        return j
    return jnp.where(j < nct, nct - 1 - j, ntl - 1 - (j - nct))


def _lru_kernel(x_ref, prev_ref, next_ref, cw_ref, cb_ref, w_ref, ba_ref, bx_ref, lam_ref,
                o_ref, a_s, u_s, h_s, *, nct, ntl, reverse):
    j = pl.program_id(1)
    tile = _lru_tile(j, nct, ntl, reverse)

    @pl.when(j == 0)
    def _():
        h_s[...] = jnp.zeros(h_s.shape, F32)

    x = x_ref[...].astype(F32)
    has_prev = jnp.logical_and(tile != 0, tile != nct)
    has_next = jnp.logical_and(tile != nct - 1, tile != ntl - 1)
    prev = jnp.where(has_prev, prev_ref[...].astype(F32), 0.0)
    nxt = jnp.where(has_next, next_ref[...].astype(F32), 0.0)
    row = lax.broadcasted_iota(jnp.int32, (TM, D), 0)
    xm1 = jnp.where(row == 0, prev[HALO - 1:HALO, :], pltpu.roll(x, 1, 0))
    xm2 = jnp.where(row == 0, prev[HALO - 2:HALO - 1, :],
                    jnp.where(row == 1, prev[HALO - 1:HALO, :], pltpu.roll(x, 2, 0)))
    xp1 = jnp.where(row == TM - 1, nxt[0:1, :], pltpu.roll(x, TM - 1, 0))
    cw = cw_ref[...]
    xl = cb_ref[...] + xm2 * cw[0:1, :] + xm1 * cw[1:2, :] + x * cw[2:3, :] + xp1 * cw[3:4, :]

    z = -lam_ref[...]
    softplus = jnp.maximum(z, 0.0) + jnp.log1p(jnp.exp(-jnp.abs(z)))
    rate = -LRU_C * softplus
    for g in range(D // LANES):
        sl = slice(g * LANES, (g + 1) * LANES)
        xg = xl[:, sl]
        pre = jnp.dot(xg.astype(BF16), w_ref[g], preferred_element_type=F32)
        r = _sigmoid(pre[:, :LANES] + ba_ref[:, sl])
        ig = _sigmoid(pre[:, LANES:] + bx_ref[:, sl])
        log_a = rate[:, sl] * r
        a = jnp.exp(log_a)
        a_s[:, sl] = a
        u_s[:, sl] = jnp.sqrt(-jnp.tanh(log_a) * (1.0 + a * a)) * (ig * xg)

    srow = lax.broadcasted_iota(jnp.int32, (SUBLANES, D), 0)
    n_chunks = TM // SUBLANES

    def chunk(c, h_prev):
        cc = (n_chunks - 1 - c) if reverse else c
        r0 = pl.multiple_of(cc * SUBLANES, SUBLANES)
        a = a_s[pl.ds(r0, SUBLANES), :]
        u = u_s[pl.ds(r0, SUBLANES), :]
        for s in (1, 2, 4):
            if reverse:
                keep = srow < SUBLANES - s
                shift = SUBLANES - s
            else:
                keep = srow >= s
                shift = s
            u = u + a * jnp.where(keep, pltpu.roll(u, shift, 0), 0.0)
            a = a * jnp.where(keep, pltpu.roll(a, shift, 0), 1.0)
        h = u + a * h_prev
        o_ref[pl.ds(r0, SUBLANES), :] = h
        edge = h[0:1, :] if reverse else h[SUBLANES - 1:SUBLANES, :]
        return jnp.broadcast_to(edge, (SUBLANES, D))

    h_s[...] = lax.fori_loop(0, n_chunks, chunk, h_s[...])


def _lru(zb, cw, cb, wd, ba, bx, lam, nct, reverse):
    b, nt, _ = zb.shape
    ntl = nt // TM
    hb = TM // HALO
    n_hb = nt // HALO
    tile = lambda j: _lru_tile(j, nct, ntl, reverse)
    kern = functools.partial(_lru_kernel, nct=nct, ntl=ntl, reverse=reverse)
    vec = lambda n: pl.BlockSpec((n, D), lambda bi, j: (0, 0))
    return pl.pallas_call(
        kern,
        out_shape=jax.ShapeDtypeStruct((b, nt, D), F32),
        grid=(b, ntl),
        in_specs=[pl.BlockSpec((None, TM, D), lambda bi, j: (bi, tile(j), 0)),
                  pl.BlockSpec((None, HALO, D),
                               lambda bi, j: (bi, jnp.maximum(tile(j) * hb - 1, 0), 0)),
                  pl.BlockSpec((None, HALO, D),
                               lambda bi, j: (bi, jnp.minimum((tile(j) + 1) * hb, n_hb - 1), 0)),
                  vec(CONV_W), vec(1),
                  pl.BlockSpec((D // LANES, LANES, 2 * LANES), lambda bi, j: (0, 0, 0)),
                  vec(1), vec(1), vec(1)],
        out_specs=pl.BlockSpec((None, TM, D), lambda bi, j: (bi, tile(j), 0)),
        scratch_shapes=[pltpu.VMEM((TM, D), F32), pltpu.VMEM((TM, D), F32),
                        pltpu.VMEM((SUBLANES, D), F32)],
        compiler_params=_cparams(("arbitrary", "arbitrary"), VMEM_LIMIT),
        name="rglru_bwd" if reverse else "rglru_fwd",
    )(zb, zb, zb, cw, cb, wd, ba, bx, lam)


def _merge_kernel(o_ref, hf_ref, hb_ref, gr_ref, ga_ref, gl_ref, x_ref, mod_ref, bg_ref, nf_ref,
                  wvo_ref, wol_ref, wout_ref, wr_ref, br_ref,
                  xn_ref, h2_ref, ri_ref, rg_ref, cnt_ref, carry_s):
    first = jnp.logical_and(pl.program_id(0) == 0, pl.program_id(1) == 0)

    @pl.when(first)
    def _():
        carry_s[...] = jnp.zeros(carry_s.shape, F32)

    gr = gr_ref[...].astype(F32)
    gelu = 0.5 * gr * (1.0 + jnp.tanh(math.sqrt(2.0 / math.pi) * (gr + 0.044715 * (gr * gr * gr))))
    lru = ((hf_ref[...] + hb_ref[...]) * gelu).astype(BF16)
    y_att = jnp.dot(o_ref[...], wvo_ref[...], preferred_element_type=F32)
    y_lru = jnp.dot(lru, wol_ref[...], preferred_element_type=F32)
    m = (_sigmoid(ga_ref[...].astype(F32) + bg_ref[0]) * y_att
         + _sigmoid(gl_ref[...].astype(F32) + bg_ref[1]) * y_lru)
    out = jnp.dot(m.astype(BF16), wout_ref[...], preferred_element_type=F32)
    xn = x_ref[...] + mod_ref[2] * out
    xn_ref[...] = xn
    h2 = _rms(xn, nf_ref[...]) * (1.0 + mod_ref[4]) + mod_ref[3]
    h2_ref[...] = h2

    logits = jnp.dot(h2, wr_ref[...], precision=lax.Precision.HIGHEST,
                     preferred_element_type=F32) + br_ref[...]
    lane = lax.broadcasted_iota(jnp.int32, (TM, LANES), 1)
    lane_f = lane.astype(F32)
    l = logits
    vals, idxs = [], []
    for _ in range(TOP_K):
        mx = jnp.max(l, axis=1, keepdims=True)
        ix = jnp.min(jnp.where(l == mx, lane_f, float(LANES)), axis=1, keepdims=True)
        vals.append(mx)
        idxs.append(ix)
        l = jnp.where(lane_f == ix, NEG_INF, l)
    chosen = jnp.where(jnp.logical_and(l == NEG_INF, lane < N_EXPERTS), 1.0, 0.0)
    es = [jnp.exp(v - vals[0]) for v in vals]
    den = es[0] + es[1] + es[2] + es[3]

    r_i = lax.broadcasted_iota(jnp.int32, (TM, TM), 0)
    c_i = lax.broadcasted_iota(jnp.int32, (TM, TM), 1)
    tri = jnp.where(r_i > c_i, 1.0, 0.0).astype(BF16)
    before = jnp.dot(tri, chosen.astype(BF16), preferred_element_type=F32) + carry_s[...]
    ri = jnp.zeros((TM, LANES), F32)
    rg = jnp.zeros((TM, LANES), F32)
    for k in range(TOP_K):
        rank = jnp.sum(jnp.where(lane_f == idxs[k], before, 0.0), axis=1, keepdims=True)
        ri = jnp.where(lane == k, idxs[k], ri)
        ri = jnp.where(lane == TOP_K + k, rank, ri)
        rg = jnp.where(lane == k, es[k] / den, rg)
    ri_ref[...] = ri.astype(jnp.int32)
    rg_ref[...] = rg
    carry = carry_s[...] + jnp.sum(chosen, axis=0, keepdims=True)
    carry_s[...] = carry
    cnt_ref[...] = jnp.broadcast_to(carry, cnt_ref.shape).astype(jnp.int32)


def _merge(o, hf, hb, zb, x, mods, bg, nf, wvo, wol, wout, wr, br, nct, tile0):
    b, nt, _ = x.shape
    ntl = nt // TM
    tok = lambda c: pl.BlockSpec((None, TM, D), lambda bi, j: (bi, j + tile0, c))
    full = lambda *s: pl.BlockSpec(s, lambda bi, j: (0,) * len(s))
    out_nt = nt - tile0 * TM
    return pl.pallas_call(
        _merge_kernel,
        out_shape=(jax.ShapeDtypeStruct((b, out_nt, D), F32),
                   jax.ShapeDtypeStruct((b, out_nt, D), F32),
                   jax.ShapeDtypeStruct((b, out_nt, LANES), jnp.int32),
                   jax.ShapeDtypeStruct((b, out_nt, LANES), F32),
                   jax.ShapeDtypeStruct((SUBLANES, LANES), jnp.int32)),
        grid=(b, ntl - tile0),
        in_specs=[tok(0), tok(0), tok(0), tok(1), tok(2), tok(3), tok(0),
                  pl.BlockSpec((None, None, 6, 1, D),
                               lambda bi, j: (bi, _seg(j + tile0, nct), 0, 0, 0)),
                  full(2, 1, D), full(1, D), full(D, D), full(D, D), full(D, D),
                  full(D, LANES), full(1, LANES)],
        out_specs=(pl.BlockSpec((None, TM, D), lambda bi, j: (bi, j, 0)),
                   pl.BlockSpec((None, TM, D), lambda bi, j: (bi, j, 0)),
                   pl.BlockSpec((None, TM, LANES), lambda bi, j: (bi, j, 0)),
                   pl.BlockSpec((None, TM, LANES), lambda bi, j: (bi, j, 0)),
                   pl.BlockSpec((SUBLANES, LANES), lambda bi, j: (0, 0))),
        scratch_shapes=[pltpu.VMEM((1, LANES), F32)],
        compiler_params=_cparams(("arbitrary", "arbitrary"), VMEM_LIMIT),
        name="merge_route",
    )(o, hf, hb, zb, zb, zb, x, mods, bg, nf, wvo, wol, wout, wr, br)


def _dispatch_kernel(meta_ref, h2_ref, dest_ref, xs_ref, idx_s, zero_s, sem, isem, *, n_blk):
    bi = pl.program_id(0)
    j = pl.program_id(1)
    first = jnp.logical_and(bi == 0, j == 0)

    @pl.when(first)
    def _():
        zero_s[...] = jnp.zeros(zero_s.shape, F32)

        def pad_copy(r):
            return pltpu.make_async_copy(zero_s.at[pl.ds(0, 1), :], xs_ref.at[pl.ds(r, 1), :], sem)

        def blk_copy(i):
            r0 = pl.multiple_of(i * MOE_R, MOE_R)
            return pltpu.make_async_copy(zero_s, xs_ref.at[pl.ds(r0, MOE_R), :], sem)

        def blk_issue(i, c):
            blk_copy(i).start()
            return c

        def blk_drain(i, c):
            blk_copy(i).wait()
            return c
        n_used = meta_ref[2 * N_EXPERTS]
        lax.fori_loop(n_used, n_blk, blk_issue, 0)
        lax.fori_loop(n_used, n_blk, blk_drain, 0)

        def per_expert(e, c):
            lo = meta_ref[e]
            hi = meta_ref[N_EXPERTS + e]

            def issue(r, c2):
                pad_copy(r).start()
                return c2

            def drain(r, c2):
                pad_copy(r).wait()
                return c2
            lax.fori_loop(lo, hi, issue, 0)
            lax.fori_loop(lo, hi, drain, 0)
            return c
        lax.fori_loop(0, N_EXPERTS, per_expert, 0)

    icp = pltpu.make_async_copy(dest_ref.at[bi, j], idx_s, isem)
    icp.start()
    icp.wait()

    def row_copy(t, k):
        return pltpu.make_async_copy(h2_ref.at[pl.ds(t, 1), :],
                                     xs_ref.at[pl.ds(idx_s[t * TOP_K + k], 1), :], sem)

    def issue(t, c):
        for k in range(TOP_K):
            row_copy(t, k).start()
        return c

    def drain(t, c):
        for k in range(TOP_K):
            row_copy(t, k).wait()
        return c
    lax.fori_loop(0, TM, issue, 0)
    lax.fori_loop(0, TM, drain, 0)


def _dispatch(pad_meta, h2, dest, n_rows):
    b, nt, _ = h2.shape
    ntl = nt // TM
    kern = functools.partial(_dispatch_kernel, n_blk=n_rows // MOE_R)
    return pl.pallas_call(
        kern,
        out_shape=jax.ShapeDtypeStruct((n_rows, D), F32),
        grid_spec=pltpu.PrefetchScalarGridSpec(
            num_scalar_prefetch=1,
            grid=(b, ntl),
            in_specs=[pl.BlockSpec((TM, D), lambda bi, j, meta: (bi * ntl + j, 0)),
                      pl.BlockSpec(memory_space=pl.ANY)],
            out_specs=pl.BlockSpec(memory_space=pl.ANY),
            scratch_shapes=[pltpu.SMEM((TM * TOP_K,), jnp.int32),
                            pltpu.VMEM((MOE_R, D), F32),
                            pltpu.SemaphoreType.DMA, pltpu.SemaphoreType.DMA]),
        compiler_params=_cparams(("arbitrary", "arbitrary"), VMEM_LIMIT),
        name="moe_dispatch",
    )(pad_meta, h2.reshape(b * nt, D), dest)


def _expert_kernel(be_ref, nu_ref, x_ref, wg_ref, bg_ref, wu_ref, bu_ref, wd_ref, bd_ref,
                   y_ref, wg_s, wu_s, wd_s):
    blk = pl.program_id(0)
    used = blk < nu_ref[0]
    prev = be_ref[jnp.maximum(blk - 1, 0)]
    fresh = jnp.logical_or(blk == 0, be_ref[blk] != prev)

    @pl.when(jnp.logical_and(used, fresh))
    def _():
        wg_s[...] = wg_ref[...].astype(BF16)
        wu_s[...] = wu_ref[...].astype(BF16)
        wd_s[...] = wd_ref[...].astype(BF16)

    @pl.when(used)
    def _():
        x = x_ref[...].astype(BF16)
        g = jnp.dot(x, wg_s[...], preferred_element_type=F32) + bg_ref[...]
        u = jnp.dot(x, wu_s[...], preferred_element_type=F32) + bu_ref[...]
        g = jnp.minimum(g, SWIGLU_LIMIT)
        u = jnp.clip(u, -SWIGLU_LIMIT, SWIGLU_LIMIT)
        act = g * _sigmoid(SWIGLU_ALPHA * g) * (u + 1.0)
        y_ref[...] = jnp.dot(act.astype(BF16), wd_s[...], preferred_element_type=F32) + bd_ref[...]

    @pl.when(jnp.logical_not(used))
    def _():
        y_ref[...] = jnp.zeros(y_ref.shape, F32)


def _experts(blk_expert, n_used, xs, wg, bg, wu, bu, wd, bd, layer):
    n_rows = xs.shape[0]
    n_blk = n_rows // MOE_R
    dff = wg.shape[-1]
    e_of = lambda i, be, nu: be[jnp.minimum(i, nu[0] - 1)]
    wspec = lambda r, c: pl.BlockSpec((None, None, r, c), lambda i, be, nu: (layer, e_of(i, be, nu), 0, 0))
    return pl.pallas_call(
        _expert_kernel,
        out_shape=jax.ShapeDtypeStruct((n_rows, D), F32),
        grid_spec=pltpu.PrefetchScalarGridSpec(
            num_scalar_prefetch=2,
            grid=(n_blk,),
            in_specs=[pl.BlockSpec((MOE_R, D), lambda i, be, nu: (jnp.minimum(i, nu[0] - 1), 0)),
                      wspec(D, dff), wspec(1, dff), wspec(D, dff), wspec(1, dff),
                      wspec(dff, D), wspec(1, D)],
            out_specs=pl.BlockSpec((MOE_R, D), lambda i, be, nu: (i, 0)),
            scratch_shapes=[pltpu.VMEM((D, dff), BF16), pltpu.VMEM((D, dff), BF16),
                            pltpu.VMEM((dff, D), BF16)]),
        compiler_params=_cparams(("arbitrary",), VMEM_LIMIT),
        name="moe_experts",
    )(blk_expert, n_used, xs, wg, bg, wu, bu, wd, bd)


def _combine_kernel(x_ref, rg_ref, mod_ref, fn_ref, dest_ref, ys_ref, o_ref, idx_s, buf_s, sem, isem,
                    *, final):
    bi = pl.program_id(0)
    j = pl.program_id(1)
    icp = pltpu.make_async_copy(dest_ref.at[bi, j], idx_s, isem)
    icp.start()
    icp.wait()

    def row_copy(t, k):
        return pltpu.make_async_copy(ys_ref.at[pl.ds(idx_s[t * TOP_K + k], 1), :],
                                     buf_s.at[k, pl.ds(t, 1), :], sem)

    def issue(t, c):
        for k in range(TOP_K):
            row_copy(t, k).start()
        return c

    def drain(t, c):
        for k in range(TOP_K):
            row_copy(t, k).wait()
        return c
    lax.fori_loop(0, TM, issue, 0)
    lax.fori_loop(0, TM, drain, 0)

    rg = rg_ref[...]
    y = rg[:, 0:1] * buf_s[0]
    for k in range(1, TOP_K):
        y = y + rg[:, k:k + 1] * buf_s[k]
    xo = x_ref[...] + mod_ref[5] * y
    if final:
        xo = _rms(xo, fn_ref[...])
    o_ref[...] = xo


def _combine(x, rg, mods, fn, dest, ys, nct, final):
    b, nt, _ = x.shape
    kern = functools.partial(_combine_kernel, final=final)
    return pl.pallas_call(
        kern,
        out_shape=jax.ShapeDtypeStruct((b, nt, D), F32),
        grid=(b, nt // TM),
        in_specs=[pl.BlockSpec((None, TM, D), lambda bi, j: (bi, j, 0)),
                  pl.BlockSpec((None, TM, LANES), lambda bi, j: (bi, j, 0)),
                  pl.BlockSpec((None, None, 6, 1, D),
                               lambda bi, j: (bi, _seg(j, nct), 0, 0, 0)),
                  pl.BlockSpec((1, D), lambda bi, j: (0, 0)),
                  pl.BlockSpec(memory_space=pl.ANY),
                  pl.BlockSpec(memory_space=pl.ANY)],
        out_specs=pl.BlockSpec((None, TM, D), lambda bi, j: (bi, j, 0)),
        scratch_shapes=[pltpu.SMEM((TM * TOP_K,), jnp.int32),
                        pltpu.VMEM((TOP_K, TM, D), F32),
                        pltpu.SemaphoreType.DMA, pltpu.SemaphoreType.DMA],
        compiler_params=_cparams(("arbitrary", "arbitrary"), VMEM_LIMIT),
        name="moe_combine",
    )(x, rg, mods, fn, dest, ys)


_ROT_PARTNER = tuple(list(range(16, 32)) + list(range(0, 16)) + list(range(48, 64)) + list(range(32, 48)))


def _rope_tables(n_ctx, n_lat):
    n = jnp.arange(n_lat)
    row = (n // GRID_W).astype(F32)
    col = (n % GRID_W).astype(F32)
    n_freq = QK_ROPE // 4
    inv = ROPE_THETA ** (-jnp.arange(n_freq, dtype=F32) / n_freq)
    ar = row[:, None] * inv
    ac = col[:, None] * inv
    c64 = jnp.concatenate([jnp.cos(ar), jnp.cos(ar), jnp.cos(ac), jnp.cos(ac)], axis=1)
    s64 = jnp.concatenate([-jnp.sin(ar), jnp.sin(ar), -jnp.sin(ac), jnp.sin(ac)], axis=1)
    c64 = jnp.concatenate([jnp.ones((n_ctx, QK_ROPE), F32), c64], axis=0)
    s64 = jnp.concatenate([jnp.zeros((n_ctx, QK_ROPE), F32), s64], axis=0)
    nt = n_ctx + n_lat
    qs = ATTN_SCALE * LOG2E
    cq = qs * jnp.concatenate([jnp.ones((nt, QK_NOPE), F32), c64, c64], axis=1)
    sq = qs * jnp.concatenate([jnp.zeros((nt, QK_NOPE), F32), s64, s64], axis=1)
    ck = jnp.concatenate([c64, s64], axis=1)
    return cq, sq, ck


def _block_diag_pairs(w):
    w = w.reshape(LRU_BLOCKS // 2, 2, LRU_BW, LRU_BW)
    z = jnp.zeros((LRU_BLOCKS // 2, LRU_BW, LRU_BW), w.dtype)
    top = jnp.concatenate([w[:, 0], z], axis=2)
    bot = jnp.concatenate([z, w[:, 1]], axis=2)
    return jnp.concatenate([top, bot], axis=1)


def kernel(x, c, ctx, c_ctx, w_ada, b_ada, norm_mix, norm_ffn, w_in, b_branch_gate, q_norm, w_uq, kv_norm, w_ukv, w_o_attn, conv_w, conv_b, lru_w_a, lru_b_a, lru_w_x, lru_b_x, lru_lambda, w_o_lru, w_out, w_router, b_router, w_exp_gate, b_exp_gate, w_exp_up, b_exp_up, w_exp_down, b_exp_down, final_norm):
    depth = w_in.shape[0]
    b, n_lat, _ = x.shape
    n_ctx = ctx.shape[1]
    assert n_ctx % TM == 0 and n_lat % TK == 0 and b + 1 <= SUBLANES
    nct = n_ctx // TM
    nt = n_ctx + n_lat
    ntl = nt // TM
    perm = jnp.array(_ROT_PARTNER)

    xs = jnp.concatenate([ctx, x], axis=1)
    cond8 = jnp.zeros((SUBLANES, D), F32).at[:b].set(c).at[b].set(c_ctx)
    cq_t, sq_t, ck_t = _rope_tables(n_ctx, n_lat)

    for l in range(depth):
        last = l == depth - 1
        tile0 = nct if last else 0

        m8 = _ada(cond8, w_ada[l], b_ada[l][None, :]).reshape(SUBLANES, 6, 1, D)
        mods = jnp.stack([jnp.broadcast_to(m8[b], (b,) + m8.shape[1:]), m8[:b]], axis=1)

        wi = w_in[l]
        kpe_w = wi[:, Q_LORA + KV_LORA:Q_LORA + KV_LORA + QK_ROPE]
        wa = jnp.concatenate([wi[:, :Q_LORA + KV_LORA + QK_ROPE], kpe_w[:, perm]], axis=1).astype(BF16)
        wb = wi[:, Q_LORA + KV_LORA + QK_ROPE:].astype(BF16)
        uq = w_uq[l].reshape(Q_LORA, HEADS, QK_DIM).transpose(1, 0, 2)
        ukv = w_ukv[l].reshape(KV_LORA, HEADS, QK_NOPE + V_HEAD).transpose(1, 0, 2)
        wqa, wvo = _fold(uq[:, :, :QK_NOPE], ukv[:, :, :QK_NOPE], ukv[:, :, QK_NOPE:],
                         w_o_attn[l].reshape(HEADS, V_HEAD, D))
        pe = uq[:, :, QK_NOPE:]
        pe_sw = pe[:, :, perm]
        zq = jnp.zeros((HEADS, Q_LORA, QK_NOPE), F32)
        wq1 = jnp.concatenate([wqa, pe, pe], axis=2)
        wq2 = jnp.concatenate([zq, pe_sw, pe_sw], axis=2)
        wq = jnp.concatenate([wq1.transpose(1, 0, 2).reshape(Q_LORA, HEADS * QH),
                              wq2.transpose(1, 0, 2).reshape(Q_LORA, HEADS * QH)], axis=1).astype(BF16)
        wvo = wvo.reshape(D, D).astype(BF16)
        wr = jnp.zeros((D, LANES), F32).at[:, :N_EXPERTS].set(w_router[l])
        br = jnp.full((1, LANES), NEG_INF, F32).at[0, :N_EXPERTS].set(b_router[l])

        za, zb = _inproj(xs, mods, norm_mix[l][None, :], wa, wb, nct)
        q, kv = _mlaproj(za, q_norm[l][None, :], kv_norm[l][None, :], wq, cq_t, sq_t, ck_t)
        o = _attention(q.reshape(b, nt * HEADS, QH), kv, n_ctx).reshape(b, nt, D)
        hs = []
        for d in range(2):
            wd = jnp.concatenate([_block_diag_pairs(lru_w_a[l, d]), _block_diag_pairs(lru_w_x[l, d])],
                                 axis=2).astype(BF16)
            hs.append(_lru(zb, conv_w[l], conv_b[l][None, :], wd, lru_b_a[l, d][None, :],
                           lru_b_x[l, d][None, :], lru_lambda[l, d][None, :], nct, reverse=d == 1))
        xn, h2, ri, rg, cnt = _merge(
            o, hs[0], hs[1], zb, xs, mods, b_branch_gate[l].reshape(2, 1, D), norm_ffn[l][None, :],
            wvo, w_o_lru[l].astype(BF16), w_out[l].astype(BF16), wr, br, nct, tile0)

        counts = cnt[0, :N_EXPERTS]
        padded = ((counts + MOE_R - 1) // MOE_R) * MOE_R
        pad_end = jnp.cumsum(padded)
        pad_start = pad_end - padded
        moe_nt = xn.shape[1]
        n_blk = (b * moe_nt * TOP_K) // MOE_R + N_EXPERTS
        n_rows = n_blk * MOE_R
        blk_start = jnp.arange(n_blk, dtype=jnp.int32) * MOE_R
        blk_expert = jnp.minimum(jnp.searchsorted(pad_end, blk_start, side="right"),
                                 N_EXPERTS - 1).astype(jnp.int32)
        n_used = (pad_end[-1:] // MOE_R).astype(jnp.int32)
        dest = (pad_start[ri[:, :, :TOP_K]] + ri[:, :, TOP_K:2 * TOP_K]).astype(jnp.int32)
        dest = dest.reshape(b, moe_nt // TM, TM * TOP_K)
        pad_meta = jnp.concatenate([pad_start + counts, pad_end, n_used]).astype(jnp.int32)

        xsort = _dispatch(pad_meta, h2, dest, n_rows)
        ys = _experts(blk_expert, n_used, xsort, w_exp_gate, b_exp_gate[:, :, None, :],
                      w_exp_up, b_exp_up[:, :, None, :], w_exp_down, b_exp_down[:, :, None, :], l)
        xs = _combine(xn, rg, mods, final_norm[None, :], dest, ys, nct - tile0, last)
    return xs
```

```python
import functools
import math

import jax
import jax.numpy as jnp
from jax import lax
from jax.experimental import pallas as pl
from jax.experimental.pallas import tpu as pltpu

D = 1024
GRID_W = 64
HEADS = 8
QK_NOPE = 128
QK_ROPE = 64
QK_DIM = QK_NOPE + QK_ROPE
V_HEAD = 128
Q_LORA = 256
KV_LORA = 128
ROPE_THETA = 10000.0
ATTN_SCALE = QK_DIM ** -0.5
LRU_BLOCKS = 16
LRU_BW = D // LRU_BLOCKS
CONV_W = 4
LRU_C = 8.0
N_EXPERTS = 32
TOP_K = 4
SWIGLU_ALPHA = 1.702
SWIGLU_LIMIT = 7.0
NORM_EPS = 1e-6
D_ZA = Q_LORA + KV_LORA + 2 * QK_ROPE
D_ZB = 4 * D
QH = 256

LANES = 128
SUBLANES = 8
TM = 256
HALO = 16
TQ = 128
TK = 1024
MOE_R = 256
VMEM_LIMIT = 56 * 1024 * 1024

F32 = jnp.float32
BF16 = jnp.bfloat16
NEG_INF = float("-inf")
LOG2E = 1.4426950408889634


def _cparams(sem, vmem=None):
    return pltpu.CompilerParams(dimension_semantics=sem, vmem_limit_bytes=vmem)


def _rms(x, g):
    return x * lax.rsqrt(jnp.mean(x * x, axis=-1, keepdims=True) + NORM_EPS) * g


def _sigmoid(x):
    return 1.0 / (1.0 + jnp.exp(-x))


def _seg(i, nct):
    return jnp.where(i < nct, 0, 1)


def _ada_kernel(c_ref, w_ref, b_ref, o_ref):
    c = c_ref[...]
    s = (c * _sigmoid(c)).astype(BF16)
    o_ref[...] = jnp.dot(s, w_ref[...].astype(BF16), preferred_element_type=F32) + b_ref[...]


def _ada(cond8, w, b, tn=768):
    n = w.shape[1]
    return pl.pallas_call(
        _ada_kernel,
        out_shape=jax.ShapeDtypeStruct((SUBLANES, n), F32),
        grid=(n // tn,),
        in_specs=[pl.BlockSpec((SUBLANES, D), lambda j: (0, 0)),
                  pl.BlockSpec((D, tn), lambda j: (0, j)),
                  pl.BlockSpec((1, tn), lambda j: (0, j))],
        out_specs=pl.BlockSpec((SUBLANES, tn), lambda j: (0, j)),
        compiler_params=_cparams(("parallel",)),
        name="adaln",
    )(cond8, w, b)


def _fold_kernel(nope_ref, wuk_ref, wuv_ref, wo_ref, wqa_ref, wvo_ref):
    hp = lax.Precision.HIGHEST
    wqa_ref[...] = lax.dot_general(nope_ref[...], wuk_ref[...], (((1,), (1,)), ((), ())),
                                   precision=hp, preferred_element_type=F32)
    wvo_ref[...] = jnp.dot(wuv_ref[...], wo_ref[...], precision=hp, preferred_element_type=F32)


def _fold(nope, wuk, wuv, wo):
    sq = lambda *s: pl.BlockSpec((None,) + s, lambda h: (h, 0, 0))
    return pl.pallas_call(
        _fold_kernel,
        out_shape=(jax.ShapeDtypeStruct((HEADS, Q_LORA, KV_LORA), F32),
                   jax.ShapeDtypeStruct((HEADS, KV_LORA, D), F32)),
        grid=(HEADS,),
        in_specs=[sq(Q_LORA, QK_NOPE), sq(KV_LORA, QK_NOPE), sq(KV_LORA, V_HEAD), sq(V_HEAD, D)],
        out_specs=(sq(Q_LORA, KV_LORA), sq(KV_LORA, D)),
        compiler_params=_cparams(("parallel",)),
        name="mla_fold",
    )(nope, wuk, wuv, wo)


def _inproj_kernel(x_ref, mod_ref, g_ref, wa_ref, wb_ref, za_ref, zb_ref):
    h = _rms(x_ref[...], g_ref[...]) * (1.0 + mod_ref[1]) + mod_ref[0]
    hb = h.astype(BF16)
    za_ref[...] = jnp.dot(hb, wa_ref[...], preferred_element_type=F32)
    for j in range(D_ZB // D):
        zb_ref[:, j * D:(j + 1) * D] = jnp.dot(
            hb, wb_ref[:, j * D:(j + 1) * D], preferred_element_type=F32).astype(BF16)


def _inproj(x, mods, g, wa, wb, nct):
    b, nt, _ = x.shape
    return pl.pallas_call(
        _inproj_kernel,
        out_shape=(jax.ShapeDtypeStruct((b, nt, D_ZA), F32),
                   jax.ShapeDtypeStruct((b, nt, D_ZB), BF16)),
        grid=(b, nt // TM),
        in_specs=[pl.BlockSpec((None, TM, D), lambda bi, i: (bi, i, 0)),
                  pl.BlockSpec((None, None, 6, 1, D), lambda bi, i: (bi, _seg(i, nct), 0, 0, 0)),
                  pl.BlockSpec((1, D), lambda bi, i: (0, 0)),
                  pl.BlockSpec((D, D_ZA), lambda bi, i: (0, 0)),
                  pl.BlockSpec((D, D_ZB), lambda bi, i: (0, 0))],
        out_specs=(pl.BlockSpec((None, TM, D_ZA), lambda bi, i: (bi, i, 0)),
                   pl.BlockSpec((None, TM, D_ZB), lambda bi, i: (bi, i, 0))),
        compiler_params=_cparams(("parallel", "parallel"), VMEM_LIMIT),
        name="in_proj",
    )(x, mods, g, wa, wb)


def _mlaproj_kernel(za_ref, qn_ref, kvn_ref, wq_ref, cq_ref, sq_ref, ck_ref, q_ref, kv_ref, v1_ref):
    za = za_ref[...]
    cq = _rms(za[:, :Q_LORA], qn_ref[...]).astype(BF16)
    r = jnp.dot(cq, wq_ref[...], preferred_element_type=F32)
    cqt = cq_ref[...]
    sqt = sq_ref[...]
    for h in range(HEADS):
        a = r[:, h * QH:(h + 1) * QH]
        s = r[:, (HEADS + h) * QH:(HEADS + h + 1) * QH]
        q_ref[:, h * QH:(h + 1) * QH] = (a * cqt + s * sqt).astype(BF16)
    ckv = _rms(za[:, Q_LORA:Q_LORA + KV_LORA], kvn_ref[...])
    kv_ref[:, :KV_LORA] = ckv.astype(BF16)
    kv_ref[:, KV_LORA:] = (za[:, Q_LORA + KV_LORA:] * ck_ref[...]).astype(BF16)
    lane = lax.broadcasted_iota(jnp.int32, (TM, LANES), 1)
    v1_ref[:, :KV_LORA] = ckv.astype(BF16)
    v1_ref[:, KV_LORA:] = jnp.where(lane == 0, 1.0, 0.0).astype(BF16)


def _mlaproj(za, qn, kvn, wq, cq, sq, ck):
    b, nt, _ = za.shape
    return pl.pallas_call(
        _mlaproj_kernel,
        out_shape=(jax.ShapeDtypeStruct((b, nt, HEADS * QH), BF16),
                   jax.ShapeDtypeStruct((b, nt, QH), BF16),
                   jax.ShapeDtypeStruct((b, nt, 2 * LANES), BF16)),
        grid=(b, nt // TM),
        in_specs=[pl.BlockSpec((None, TM, D_ZA), lambda bi, i: (bi, i, 0)),
                  pl.BlockSpec((1, Q_LORA), lambda bi, i: (0, 0)),
                  pl.BlockSpec((1, KV_LORA), lambda bi, i: (0, 0)),
                  pl.BlockSpec((Q_LORA, 2 * HEADS * QH), lambda bi, i: (0, 0)),
                  pl.BlockSpec((TM, QH), lambda bi, i: (i, 0)),
                  pl.BlockSpec((TM, QH), lambda bi, i: (i, 0)),
                  pl.BlockSpec((TM, 2 * QK_ROPE), lambda bi, i: (i, 0))],
        out_specs=(pl.BlockSpec((None, TM, HEADS * QH), lambda bi, i: (bi, i, 0)),
                   pl.BlockSpec((None, TM, QH), lambda bi, i: (bi, i, 0)),
                   pl.BlockSpec((None, TM, 2 * LANES), lambda bi, i: (bi, i, 0))),
        compiler_params=_cparams(("parallel", "parallel"), VMEM_LIMIT),
        name="mla_proj",
    )(za, qn, kvn, wq, cq, sq, ck)


def _attn_kernel(q_ref, kv_ref, v1_ref, o_ref, q_s, m_s, acc_s, s_a, s_b, s_c,
                 *, n_ctx, n_lat, n_ctx_qtiles):
    i = pl.program_id(1)
    m_s[...] = jnp.full(m_s.shape, NEG_INF, F32)
    acc_s[...] = jnp.zeros(acc_s.shape, F32)
    for h in range(HEADS):
        q_s[h * TQ:(h + 1) * TQ, :] = q_ref[:, h * QH:(h + 1) * QH]

    def scores(s_ref, k0, size):
        k = kv_ref[pl.ds(k0, size), :]
        s_ref[...] = lax.dot_general(q_s[...], k, (((1,), (1,)), ((), ())), preferred_element_type=F32)

    def softmax_pv(s_ref, k0, size):
        s = s_ref[...]
        m_old = m_s[...]
        m_new = jnp.maximum(m_old, jnp.max(s, axis=1, keepdims=True))
        alpha = jnp.exp2(m_old - m_new)
        p = jnp.exp2(s - m_new).astype(BF16)
        acc_s[...] = alpha * acc_s[...] + jnp.dot(p, v1_ref[pl.ds(k0, size), :],
                                                  preferred_element_type=F32)
        m_s[...] = m_new

    @pl.when(i < n_ctx_qtiles)
    def _():
        scores(s_c, 0, n_ctx)
        softmax_pv(s_c, 0, n_ctx)

    sizes = [TK] * (n_lat // TK)
    tiles, k0 = [(s_c, 0, n_ctx)], n_ctx
    for t, size in enumerate(sizes):
        tiles.append(((s_a, s_b)[t % 2].at[:, :size], k0, size))
        k0 += size

    @pl.when(i >= n_ctx_qtiles)
    def _():
        scores(*tiles[0])
        for t, tile in enumerate(tiles):
            if t + 1 < len(tiles):
                scores(*tiles[t + 1])
            softmax_pv(*tile)

    acc = acc_s[...]
    o = (acc[:, :KV_LORA] / acc[:, KV_LORA:KV_LORA + 1]).astype(BF16)
    for h in range(HEADS):
        o_ref[:, h * KV_LORA:(h + 1) * KV_LORA] = o[h * TQ:(h + 1) * TQ, :]


def _attention(q, kv, v1, n_ctx):
    b, nt, _ = kv.shape
    rows = TQ * HEADS
    n_lat = nt - n_ctx
    assert n_lat % TK == 0
    kern = functools.partial(_attn_kernel, n_ctx=n_ctx, n_lat=n_lat, n_ctx_qtiles=n_ctx // TQ)
    return pl.pallas_call(
        kern,
        out_shape=jax.ShapeDtypeStruct((b, nt, HEADS * KV_LORA), BF16),
        grid=(b, nt // TQ),
        in_specs=[pl.BlockSpec((None, TQ, HEADS * QH), lambda bi, i: (bi, i, 0)),
                  pl.BlockSpec((None, nt, QH), lambda bi, i: (bi, 0, 0)),
                  pl.BlockSpec((None, nt, 2 * LANES), lambda bi, i: (bi, 0, 0))],
        out_specs=pl.BlockSpec((None, TQ, HEADS * KV_LORA), lambda bi, i: (bi, i, 0)),
        scratch_shapes=[pltpu.VMEM((rows, QH), BF16),
                        pltpu.VMEM((rows, 1), F32), pltpu.VMEM((rows, 2 * LANES), F32),
                        pltpu.VMEM((rows, TK), F32), pltpu.VMEM((rows, TK), F32),
                        pltpu.VMEM((rows, n_ctx), F32)],
        compiler_params=_cparams(("parallel", "arbitrary"), VMEM_LIMIT),
        name="mla_attention",
    )(q, kv, v1)


def _lru_tile(j, nct, ntl, reverse):
    if not reverse:
        return j
    return jnp.where(j < nct, nct - 1 - j, ntl - 1 - (j - nct))


def _lru_kernel(x_ref, prev_ref, next_ref, cw_ref, cb_ref, w_ref, ba_ref, bx_ref, lam_ref,
                o_ref, a_s, u_s, h_s, *, nct, ntl, reverse):
    j = pl.program_id(1)
    tile = _lru_tile(j, nct, ntl, reverse)

    @pl.when(j == 0)
    def _():
        h_s[...] = jnp.zeros(h_s.shape, F32)

    x = x_ref[...].astype(F32)
    has_prev = jnp.logical_and(tile != 0, tile != nct)
    has_next = jnp.logical_and(tile != nct - 1, tile != ntl - 1)
    prev = jnp.where(has_prev, prev_ref[...].astype(F32), 0.0)
    nxt = jnp.where(has_next, next_ref[...].astype(F32), 0.0)
    row = lax.broadcasted_iota(jnp.int32, (TM, D), 0)
    xm1 = jnp.where(row == 0, prev[HALO - 1:HALO, :], pltpu.roll(x, 1, 0))
    xm2 = jnp.where(row == 0, prev[HALO - 2:HALO - 1, :],
                    jnp.where(row == 1, prev[HALO - 1:HALO, :], pltpu.roll(x, 2, 0)))
    xp1 = jnp.where(row == TM - 1, nxt[0:1, :], pltpu.roll(x, TM - 1, 0))
    cw = cw_ref[...]
    xl = cb_ref[...] + xm2 * cw[0:1, :] + xm1 * cw[1:2, :] + x * cw[2:3, :] + xp1 * cw[3:4, :]

    z = -lam_ref[...]
    softplus = jnp.maximum(z, 0.0) + jnp.log1p(jnp.exp(-jnp.abs(z)))
    rate = -LRU_C * softplus
    for g in range(D // LANES):
        sl = slice(g * LANES, (g + 1) * LANES)
        xg = xl[:, sl]
        pre = jnp.dot(xg.astype(BF16), w_ref[g], preferred_element_type=F32)
        r = _sigmoid(pre[:, :LANES] + ba_ref[:, sl])
        ig = _sigmoid(pre[:, LANES:] + bx_ref[:, sl])
        log_a = rate[:, sl] * r
        a = jnp.exp(log_a)
        a_s[:, sl] = a
        u_s[:, sl] = jnp.sqrt(-jnp.tanh(log_a) * (1.0 + a * a)) * (ig * xg)

    srow = lax.broadcasted_iota(jnp.int32, (SUBLANES, D), 0)
    n_chunks = TM // SUBLANES

    def chunk(c, h_prev):
        cc = (n_chunks - 1 - c) if reverse else c
        r0 = pl.multiple_of(cc * SUBLANES, SUBLANES)
        a = a_s[pl.ds(r0, SUBLANES), :]
        u = u_s[pl.ds(r0, SUBLANES), :]
        for s in (1, 2, 4):
            if reverse:
                keep = srow < SUBLANES - s
                shift = SUBLANES - s
            else:
                keep = srow >= s
                shift = s
            u = u + a * jnp.where(keep, pltpu.roll(u, shift, 0), 0.0)
            a = a * jnp.where(keep, pltpu.roll(a, shift, 0), 1.0)
        h = u + a * h_prev
        o_ref[pl.ds(r0, SUBLANES), :] = h
        edge = h[0:1, :] if reverse else h[SUBLANES - 1:SUBLANES, :]
        return jnp.broadcast_to(edge, (SUBLANES, D))

    h_s[...] = lax.fori_loop(0, n_chunks, chunk, h_s[...])


def _lru(zb, cw, cb, wd, ba, bx, lam, nct, reverse):
    b, nt, _ = zb.shape
    ntl = nt // TM
    hb = TM // HALO
    n_hb = nt // HALO
    tile = lambda j: _lru_tile(j, nct, ntl, reverse)
    kern = functools.partial(_lru_kernel, nct=nct, ntl=ntl, reverse=reverse)
    vec = lambda n: pl.BlockSpec((n, D), lambda bi, j: (0, 0))
    return pl.pallas_call(
        kern,
        out_shape=jax.ShapeDtypeStruct((b, nt, D), F32),
        grid=(b, ntl),
        in_specs=[pl.BlockSpec((None, TM, D), lambda bi, j: (bi, tile(j), 0)),
                  pl.BlockSpec((None, HALO, D),
                               lambda bi, j: (bi, jnp.maximum(tile(j) * hb - 1, 0), 0)),
                  pl.BlockSpec((None, HALO, D),
                               lambda bi, j: (bi, jnp.minimum((tile(j) + 1) * hb, n_hb - 1), 0)),
                  vec(CONV_W), vec(1),
                  pl.BlockSpec((D // LANES, LANES, 2 * LANES), lambda bi, j: (0, 0, 0)),
                  vec(1), vec(1), vec(1)],
        out_specs=pl.BlockSpec((None, TM, D), lambda bi, j: (bi, tile(j), 0)),
        scratch_shapes=[pltpu.VMEM((TM, D), F32), pltpu.VMEM((TM, D), F32),
                        pltpu.VMEM((SUBLANES, D), F32)],
        compiler_params=_cparams(("arbitrary", "arbitrary"), VMEM_LIMIT),
        name="rglru_bwd" if reverse else "rglru_fwd",
    )(zb, zb, zb, cw, cb, wd, ba, bx, lam)


def _merge_kernel(o_ref, hf_ref, hb_ref, gr_ref, ga_ref, gl_ref, x_ref, mod_ref, bg_ref, nf_ref,
                  wvo_ref, wol_ref, wout_ref, wr_ref, br_ref,
                  xn_ref, h2_ref, ri_ref, rg_ref, cnt_ref, carry_s):
    first = jnp.logical_and(pl.program_id(0) == 0, pl.program_id(1) == 0)

    @pl.when(first)
    def _():
        carry_s[...] = jnp.zeros(carry_s.shape, F32)

    gr = gr_ref[...].astype(F32)
    gelu = 0.5 * gr * (1.0 + jnp.tanh(math.sqrt(2.0 / math.pi) * (gr + 0.044715 * (gr * gr * gr))))
    lru = ((hf_ref[...] + hb_ref[...]) * gelu).astype(BF16)
    y_att = jnp.dot(o_ref[...], wvo_ref[...], preferred_element_type=F32)
    y_lru = jnp.dot(lru, wol_ref[...], preferred_element_type=F32)
    m = (_sigmoid(ga_ref[...].astype(F32) + bg_ref[0]) * y_att
         + _sigmoid(gl_ref[...].astype(F32) + bg_ref[1]) * y_lru)
    out = jnp.dot(m.astype(BF16), wout_ref[...], preferred_element_type=F32)
    xn = x_ref[...] + mod_ref[2] * out
    xn_ref[...] = xn
    h2 = _rms(xn, nf_ref[...]) * (1.0 + mod_ref[4]) + mod_ref[3]
    h2_ref[...] = h2

    wr = wr_ref[...]
    h_hi = h2.astype(BF16)
    h_lo = (h2 - h_hi.astype(F32)).astype(BF16)
    w_hi = wr.astype(BF16)
    w_lo = (wr - w_hi.astype(F32)).astype(BF16)
    logits = (jnp.dot(h_hi, w_hi, preferred_element_type=F32)
              + (jnp.dot(h_lo, w_hi, preferred_element_type=F32)
                 + jnp.dot(h_hi, w_lo, preferred_element_type=F32))) + br_ref[...]
    lane = lax.broadcasted_iota(jnp.int32, (TM, LANES), 1)
    lane_f = lane.astype(F32)
    l = logits
    vals, idxs = [], []
    for _ in range(TOP_K):
        mx = jnp.max(l, axis=1, keepdims=True)
        ix = jnp.min(jnp.where(l == mx, lane_f, float(LANES)), axis=1, keepdims=True)
        vals.append(mx)
        idxs.append(ix)
        l = jnp.where(lane_f == ix, NEG_INF, l)
    chosen = jnp.where(jnp.logical_and(l == NEG_INF, lane < N_EXPERTS), 1.0, 0.0)
    es = [jnp.exp(v - vals[0]) for v in vals]
    den = es[0] + es[1] + es[2] + es[3]

    r_i = lax.broadcasted_iota(jnp.int32, (TM, TM), 0)
    c_i = lax.broadcasted_iota(jnp.int32, (TM, TM), 1)
    tri = jnp.where(r_i > c_i, 1.0, 0.0).astype(BF16)
    before = jnp.dot(tri, chosen.astype(BF16), preferred_element_type=F32) + carry_s[...]
    ri = jnp.zeros((TM, LANES), F32)
    rg = jnp.zeros((TM, LANES), F32)
    for k in range(TOP_K):
        rank = jnp.sum(jnp.where(lane_f == idxs[k], before, 0.0), axis=1, keepdims=True)
        ri = jnp.where(lane == k, idxs[k], ri)
        ri = jnp.where(lane == TOP_K + k, rank, ri)
        rg = jnp.where(lane == k, es[k] / den, rg)
    ri_ref[...] = ri.astype(jnp.int32)
    rg_ref[...] = rg
    carry = carry_s[...] + jnp.sum(chosen, axis=0, keepdims=True)
    carry_s[...] = carry
    cnt_ref[...] = jnp.broadcast_to(carry, cnt_ref.shape).astype(jnp.int32)


def _merge(o, hf, hb, zb, x, mods, bg, nf, wvo, wol, wout, wr, br, nct, tile0):
    b, nt, _ = x.shape
    ntl = nt // TM
    tok = lambda c: pl.BlockSpec((None, TM, D), lambda bi, j: (bi, j + tile0, c))
    full = lambda *s: pl.BlockSpec(s, lambda bi, j: (0,) * len(s))
    out_nt = nt - tile0 * TM
    return pl.pallas_call(
        _merge_kernel,
        out_shape=(jax.ShapeDtypeStruct((b, out_nt, D), F32),
                   jax.ShapeDtypeStruct((b, out_nt, D), F32),
                   jax.ShapeDtypeStruct((b, out_nt, LANES), jnp.int32),
                   jax.ShapeDtypeStruct((b, out_nt, LANES), F32),
                   jax.ShapeDtypeStruct((SUBLANES, LANES), jnp.int32)),
        grid=(b, ntl - tile0),
        in_specs=[tok(0), tok(0), tok(0), tok(1), tok(2), tok(3), tok(0),
                  pl.BlockSpec((None, None, 6, 1, D),
                               lambda bi, j: (bi, _seg(j + tile0, nct), 0, 0, 0)),
                  full(2, 1, D), full(1, D), full(D, D), full(D, D), full(D, D),
                  full(D, LANES), full(1, LANES)],
        out_specs=(pl.BlockSpec((None, TM, D), lambda bi, j: (bi, j, 0)),
                   pl.BlockSpec((None, TM, D), lambda bi, j: (bi, j, 0)),
                   pl.BlockSpec((None, TM, LANES), lambda bi, j: (bi, j, 0)),
                   pl.BlockSpec((None, TM, LANES), lambda bi, j: (bi, j, 0)),
                   pl.BlockSpec((SUBLANES, LANES), lambda bi, j: (0, 0))),
        scratch_shapes=[pltpu.VMEM((1, LANES), F32)],
        compiler_params=_cparams(("arbitrary", "arbitrary"), VMEM_LIMIT),
        name="merge_route",
    )(o, hf, hb, zb, zb, zb, x, mods, bg, nf, wvo, wol, wout, wr, br)


def _dispatch_kernel(meta_ref, h2_ref, dest_ref, xs_ref, idx_s, zero_s, sem, isem, *, n_blk):
    bi = pl.program_id(0)
    j = pl.program_id(1)
    first = jnp.logical_and(bi == 0, j == 0)

    @pl.when(first)
    def _():
        zero_s[...] = jnp.zeros(zero_s.shape, F32)

        def pad_copy(r):
            return pltpu.make_async_copy(zero_s.at[pl.ds(0, 1), :], xs_ref.at[pl.ds(r, 1), :], sem)

        def blk_copy(i):
            r0 = pl.multiple_of(i * MOE_R, MOE_R)
            return pltpu.make_async_copy(zero_s, xs_ref.at[pl.ds(r0, MOE_R), :], sem)

        def blk_issue(i, c):
            blk_copy(i).start()
            return c

        def blk_drain(i, c):
            blk_copy(i).wait()
            return c
        n_used = meta_ref[2 * N_EXPERTS]
        lax.fori_loop(n_used, n_blk, blk_issue, 0)
        lax.fori_loop(n_used, n_blk, blk_drain, 0)

        def per_expert(e, c):
            lo = meta_ref[e]
            hi = meta_ref[N_EXPERTS + e]

            def issue(r, c2):
                pad_copy(r).start()
                return c2

            def drain(r, c2):
                pad_copy(r).wait()
                return c2
            lax.fori_loop(lo, hi, issue, 0)
            lax.fori_loop(lo, hi, drain, 0)
            return c
        lax.fori_loop(0, N_EXPERTS, per_expert, 0)

    icp = pltpu.make_async_copy(dest_ref.at[bi, j], idx_s, isem)
    icp.start()
    icp.wait()

    def row_copy(t, k):
        return pltpu.make_async_copy(h2_ref.at[pl.ds(t, 1), :],
                                     xs_ref.at[pl.ds(idx_s[t * TOP_K + k], 1), :], sem)

    def issue(t, c):
        for k in range(TOP_K):
            row_copy(t, k).start()
        return c

    def drain(t, c):
        for k in range(TOP_K):
            row_copy(t, k).wait()
        return c
    lax.fori_loop(0, TM, issue, 0)
    lax.fori_loop(0, TM, drain, 0)


def _dispatch(pad_meta, h2, dest, n_rows):
    b, nt, _ = h2.shape
    ntl = nt // TM
    kern = functools.partial(_dispatch_kernel, n_blk=n_rows // MOE_R)
    return pl.pallas_call(
        kern,
        out_shape=jax.ShapeDtypeStruct((n_rows, D), F32),
        grid_spec=pltpu.PrefetchScalarGridSpec(
            num_scalar_prefetch=1,
            grid=(b, ntl),
            in_specs=[pl.BlockSpec((TM, D), lambda bi, j, meta: (bi * ntl + j, 0)),
                      pl.BlockSpec(memory_space=pl.ANY)],
            out_specs=pl.BlockSpec(memory_space=pl.ANY),
            scratch_shapes=[pltpu.SMEM((TM * TOP_K,), jnp.int32),
                            pltpu.VMEM((MOE_R, D), F32),
                            pltpu.SemaphoreType.DMA, pltpu.SemaphoreType.DMA]),
        compiler_params=_cparams(("arbitrary", "arbitrary"), VMEM_LIMIT),
        name="moe_dispatch",
    )(pad_meta, h2.reshape(b * nt, D), dest)


def _expert_kernel(be_ref, nu_ref, x_ref, wg_ref, bg_ref, wu_ref, bu_ref, wd_ref, bd_ref,
                   y_ref, wg_s, wu_s, wd_s):
    blk = pl.program_id(0)
    used = blk < nu_ref[0]
    prev = be_ref[jnp.maximum(blk - 1, 0)]
    fresh = jnp.logical_or(blk == 0, be_ref[blk] != prev)

    @pl.when(jnp.logical_and(used, fresh))
    def _():
        wg_s[...] = wg_ref[...].astype(BF16)
        wu_s[...] = wu_ref[...].astype(BF16)
        wd_s[...] = wd_ref[...].astype(BF16)

    @pl.when(used)
    def _():
        x = x_ref[...].astype(BF16)
        g = jnp.dot(x, wg_s[...], preferred_element_type=F32) + bg_ref[...]
        u = jnp.dot(x, wu_s[...], preferred_element_type=F32) + bu_ref[...]
        g = jnp.minimum(g, SWIGLU_LIMIT)
        u = jnp.clip(u, -SWIGLU_LIMIT, SWIGLU_LIMIT)
        act = g * _sigmoid(SWIGLU_ALPHA * g) * (u + 1.0)
        y_ref[...] = jnp.dot(act.astype(BF16), wd_s[...], preferred_element_type=F32) + bd_ref[...]

    @pl.when(jnp.logical_not(used))
    def _():
        y_ref[...] = jnp.zeros(y_ref.shape, F32)


def _experts(blk_expert, n_used, xs, wg, bg, wu, bu, wd, bd, layer):
    n_rows = xs.shape[0]
    n_blk = n_rows // MOE_R
    dff = wg.shape[-1]
    e_of = lambda i, be, nu: be[jnp.minimum(i, nu[0] - 1)]
    wspec = lambda r, c: pl.BlockSpec((None, None, r, c), lambda i, be, nu: (layer, e_of(i, be, nu), 0, 0))
    return pl.pallas_call(
        _expert_kernel,
        out_shape=jax.ShapeDtypeStruct((n_rows, D), F32),
        grid_spec=pltpu.PrefetchScalarGridSpec(
            num_scalar_prefetch=2,
            grid=(n_blk,),
            in_specs=[pl.BlockSpec((MOE_R, D), lambda i, be, nu: (jnp.minimum(i, nu[0] - 1), 0)),
                      wspec(D, dff), wspec(1, dff), wspec(D, dff), wspec(1, dff),
                      wspec(dff, D), wspec(1, D)],
            out_specs=pl.BlockSpec((MOE_R, D), lambda i, be, nu: (i, 0)),
            scratch_shapes=[pltpu.VMEM((D, dff), BF16), pltpu.VMEM((D, dff), BF16),
                            pltpu.VMEM((dff, D), BF16)]),
        compiler_params=_cparams(("arbitrary",), VMEM_LIMIT),
        name="moe_experts",
    )(blk_expert, n_used, xs, wg, bg, wu, bu, wd, bd)


def _combine_kernel(x_ref, rg_ref, mod_ref, fn_ref, dest_ref, ys_ref, o_ref, idx_s, buf_s, sem, isem,
                    *, final):
    bi = pl.program_id(0)
    j = pl.program_id(1)
    icp = pltpu.make_async_copy(dest_ref.at[bi, j], idx_s, isem)
    icp.start()
    icp.wait()

    def row_copy(t, k):
        return pltpu.make_async_copy(ys_ref.at[pl.ds(idx_s[t * TOP_K + k], 1), :],
                                     buf_s.at[k, pl.ds(t, 1), :], sem)

    def issue(t, c):
        for k in range(TOP_K):
            row_copy(t, k).start()
        return c

    def drain(t, c):
        for k in range(TOP_K):
            row_copy(t, k).wait()
        return c
    lax.fori_loop(0, TM, issue, 0)
    lax.fori_loop(0, TM, drain, 0)

    rg = rg_ref[...]
    y = rg[:, 0:1] * buf_s[0]
    for k in range(1, TOP_K):
        y = y + rg[:, k:k + 1] * buf_s[k]
    xo = x_ref[...] + mod_ref[5] * y
    if final:
        xo = _rms(xo, fn_ref[...])
    o_ref[...] = xo


def _combine(x, rg, mods, fn, dest, ys, nct, final):
    b, nt, _ = x.shape
    kern = functools.partial(_combine_kernel, final=final)
    return pl.pallas_call(
        kern,
        out_shape=jax.ShapeDtypeStruct((b, nt, D), F32),
        grid=(b, nt // TM),
        in_specs=[pl.BlockSpec((None, TM, D), lambda bi, j: (bi, j, 0)),
                  pl.BlockSpec((None, TM, LANES), lambda bi, j: (bi, j, 0)),
                  pl.BlockSpec((None, None, 6, 1, D),
                               lambda bi, j: (bi, _seg(j, nct), 0, 0, 0)),
                  pl.BlockSpec((1, D), lambda bi, j: (0, 0)),
                  pl.BlockSpec(memory_space=pl.ANY),
                  pl.BlockSpec(memory_space=pl.ANY)],
        out_specs=pl.BlockSpec((None, TM, D), lambda bi, j: (bi, j, 0)),
        scratch_shapes=[pltpu.SMEM((TM * TOP_K,), jnp.int32),
                        pltpu.VMEM((TOP_K, TM, D), F32),
                        pltpu.SemaphoreType.DMA, pltpu.SemaphoreType.DMA],
        compiler_params=_cparams(("arbitrary", "arbitrary"), VMEM_LIMIT),
        name="moe_combine",
    )(x, rg, mods, fn, dest, ys)


_ROT_PARTNER = tuple(list(range(16, 32)) + list(range(0, 16)) + list(range(48, 64)) + list(range(32, 48)))


def _rope_tables(n_ctx, n_lat):
    n = jnp.arange(n_lat)
    row = (n // GRID_W).astype(F32)
    col = (n % GRID_W).astype(F32)
    n_freq = QK_ROPE // 4
    inv = ROPE_THETA ** (-jnp.arange(n_freq, dtype=F32) / n_freq)
    ar = row[:, None] * inv
    ac = col[:, None] * inv
    c64 = jnp.concatenate([jnp.cos(ar), jnp.cos(ar), jnp.cos(ac), jnp.cos(ac)], axis=1)
    s64 = jnp.concatenate([-jnp.sin(ar), jnp.sin(ar), -jnp.sin(ac), jnp.sin(ac)], axis=1)
    c64 = jnp.concatenate([jnp.ones((n_ctx, QK_ROPE), F32), c64], axis=0)
    s64 = jnp.concatenate([jnp.zeros((n_ctx, QK_ROPE), F32), s64], axis=0)
    nt = n_ctx + n_lat
    qs = ATTN_SCALE * LOG2E
    cq = qs * jnp.concatenate([jnp.ones((nt, QK_NOPE), F32), c64, c64], axis=1)
    sq = qs * jnp.concatenate([jnp.zeros((nt, QK_NOPE), F32), s64, s64], axis=1)
    ck = jnp.concatenate([c64, s64], axis=1)
    return cq, sq, ck


def _block_diag_pairs(w):
    w = w.reshape(LRU_BLOCKS // 2, 2, LRU_BW, LRU_BW)
    z = jnp.zeros((LRU_BLOCKS // 2, LRU_BW, LRU_BW), w.dtype)
    top = jnp.concatenate([w[:, 0], z], axis=2)
    bot = jnp.concatenate([z, w[:, 1]], axis=2)
    return jnp.concatenate([top, bot], axis=1)


def kernel(x, c, ctx, c_ctx, w_ada, b_ada, norm_mix, norm_ffn, w_in, b_branch_gate, q_norm, w_uq, kv_norm, w_ukv, w_o_attn, conv_w, conv_b, lru_w_a, lru_b_a, lru_w_x, lru_b_x, lru_lambda, w_o_lru, w_out, w_router, b_router, w_exp_gate, b_exp_gate, w_exp_up, b_exp_up, w_exp_down, b_exp_down, final_norm):
    depth = w_in.shape[0]
    b, n_lat, _ = x.shape
    n_ctx = ctx.shape[1]
    assert n_ctx % TM == 0 and n_lat % TK == 0 and b + 1 <= SUBLANES
    nct = n_ctx // TM
    nt = n_ctx + n_lat
    ntl = nt // TM
    perm = jnp.array(_ROT_PARTNER)

    xs = jnp.concatenate([ctx, x], axis=1)
    cond8 = jnp.zeros((SUBLANES, D), F32).at[:b].set(c).at[b].set(c_ctx)
    cq_t, sq_t, ck_t = _rope_tables(n_ctx, n_lat)

    for l in range(depth):
        last = l == depth - 1
        tile0 = nct if last else 0

        m8 = _ada(cond8, w_ada[l], b_ada[l][None, :]).reshape(SUBLANES, 6, 1, D)
        mods = jnp.stack([jnp.broadcast_to(m8[b], (b,) + m8.shape[1:]), m8[:b]], axis=1)

        wi = w_in[l]
        kpe_w = wi[:, Q_LORA + KV_LORA:Q_LORA + KV_LORA + QK_ROPE]
        wa = jnp.concatenate([wi[:, :Q_LORA + KV_LORA + QK_ROPE], kpe_w[:, perm]], axis=1).astype(BF16)
        wb = wi[:, Q_LORA + KV_LORA + QK_ROPE:].astype(BF16)
        uq = w_uq[l].reshape(Q_LORA, HEADS, QK_DIM).transpose(1, 0, 2)
        ukv = w_ukv[l].reshape(KV_LORA, HEADS, QK_NOPE + V_HEAD).transpose(1, 0, 2)
        wqa, wvo = _fold(uq[:, :, :QK_NOPE], ukv[:, :, :QK_NOPE], ukv[:, :, QK_NOPE:],
                         w_o_attn[l].reshape(HEADS, V_HEAD, D))
        pe = uq[:, :, QK_NOPE:]
        pe_sw = pe[:, :, perm]
        zq = jnp.zeros((HEADS, Q_LORA, QK_NOPE), F32)
        wq1 = jnp.concatenate([wqa, pe, pe], axis=2)
        wq2 = jnp.concatenate([zq, pe_sw, pe_sw], axis=2)
        wq = jnp.concatenate([wq1.transpose(1, 0, 2).reshape(Q_LORA, HEADS * QH),
                              wq2.transpose(1, 0, 2).reshape(Q_LORA, HEADS * QH)], axis=1).astype(BF16)
        wvo = wvo.reshape(D, D).astype(BF16)
        wr = jnp.zeros((D, LANES), F32).at[:, :N_EXPERTS].set(w_router[l])
        br = jnp.full((1, LANES), NEG_INF, F32).at[0, :N_EXPERTS].set(b_router[l])

        za, zb = _inproj(xs, mods, norm_mix[l][None, :], wa, wb, nct)
        q, kv, v1 = _mlaproj(za, q_norm[l][None, :], kv_norm[l][None, :], wq, cq_t, sq_t, ck_t)
        o = _attention(q, kv, v1, n_ctx)
        hs = []
        for d in range(2):
            wd = jnp.concatenate([_block_diag_pairs(lru_w_a[l, d]), _block_diag_pairs(lru_w_x[l, d])],
                                 axis=2).astype(BF16)
            hs.append(_lru(zb, conv_w[l], conv_b[l][None, :], wd, lru_b_a[l, d][None, :],
                           lru_b_x[l, d][None, :], lru_lambda[l, d][None, :], nct, reverse=d == 1))
        xn, h2, ri, rg, cnt = _merge(
            o, hs[0], hs[1], zb, xs, mods, b_branch_gate[l].reshape(2, 1, D), norm_ffn[l][None, :],
            wvo, w_o_lru[l].astype(BF16), w_out[l].astype(BF16), wr, br, nct, tile0)

        counts = cnt[0, :N_EXPERTS]
        padded = ((counts + MOE_R - 1) // MOE_R) * MOE_R
        pad_end = jnp.cumsum(padded)
        pad_start = pad_end - padded
        moe_nt = xn.shape[1]
        n_blk = (b * moe_nt * TOP_K) // MOE_R + N_EXPERTS
        n_rows = n_blk * MOE_R
        blk_start = jnp.arange(n_blk, dtype=jnp.int32) * MOE_R
        blk_expert = jnp.minimum(jnp.sum(blk_start[:, None] >= pad_end[None, :], axis=1),
                                 N_EXPERTS - 1).astype(jnp.int32)
        n_used = (pad_end[-1:] // MOE_R).astype(jnp.int32)
        e_ids = jnp.arange(N_EXPERTS, dtype=jnp.int32)
        start_of = jnp.sum(jnp.where(ri[:, :, :TOP_K, None] == e_ids, pad_start, 0), axis=-1)
        dest = (start_of + ri[:, :, TOP_K:2 * TOP_K]).astype(jnp.int32)
        dest = dest.reshape(b, moe_nt // TM, TM * TOP_K)
        pad_meta = jnp.concatenate([pad_start + counts, pad_end, n_used]).astype(jnp.int32)

        xsort = _dispatch(pad_meta, h2, dest, n_rows)
        ys = _experts(blk_expert, n_used, xsort, w_exp_gate, b_exp_gate[:, :, None, :],
                      w_exp_up, b_exp_up[:, :, None, :], w_exp_down, b_exp_down[:, :, None, :], l)
        xs = _combine(xn, rg, mods, final_norm[None, :], dest, ys, nct - tile0, last)
    return xs
```

```python
import functools
import math

import jax
import jax.numpy as jnp
from jax import lax
from jax.experimental import pallas as pl
from jax.experimental.pallas import tpu as pltpu

D = 1024
GRID_W = 64
HEADS = 8
QK_NOPE = 128
QK_ROPE = 64
QK_DIM = QK_NOPE + QK_ROPE
V_HEAD = 128
Q_LORA = 256
KV_LORA = 128
ROPE_THETA = 10000.0
ATTN_SCALE = QK_DIM ** -0.5
LRU_BLOCKS = 16
LRU_BW = D // LRU_BLOCKS
CONV_W = 4
LRU_C = 8.0
N_EXPERTS = 32
TOP_K = 4
SWIGLU_ALPHA = 1.702
SWIGLU_LIMIT = 7.0
NORM_EPS = 1e-6
D_ZA = Q_LORA + KV_LORA + 2 * QK_ROPE
D_ZB = 4 * D
QH = 256

LANES = 128
SUBLANES = 8
TM = 256
HALO = 16
TQ = 128
ATT_CHAINS = 1
TK = 1024
MOE_R = 256
MOE_CH = 16
MOE_SLOTS = TM * TOP_K + N_EXPERTS * MOE_CH
META_TAIL = N_EXPERTS
META_END = 2 * N_EXPERTS
META_NUSED = 3 * N_EXPERTS
META_TILES = 4 * N_EXPERTS
VMEM_LIMIT = 56 * 1024 * 1024

F32 = jnp.float32
BF16 = jnp.bfloat16
NEG_INF = float("-inf")
LOG2E = 1.4426950408889634


def _cparams(sem, vmem=None):
    return pltpu.CompilerParams(dimension_semantics=sem, vmem_limit_bytes=vmem)


def _rms(x, g):
    return x * lax.rsqrt(jnp.mean(x * x, axis=-1, keepdims=True) + NORM_EPS) * g


def _sigmoid(x):
    return 1.0 / (1.0 + jnp.exp(-x))


def _seg(i, nct):
    return jnp.where(i < nct, 0, 1)


def _ada_kernel(c_ref, w_ref, b_ref, o_ref):
    c = c_ref[...]
    s = (c * _sigmoid(c)).astype(BF16)
    o_ref[...] = jnp.dot(s, w_ref[...].astype(BF16), preferred_element_type=F32) + b_ref[...]


def _ada(cond8, w, b, tn=768):
    n = w.shape[1]
    return pl.pallas_call(
        _ada_kernel,
        out_shape=jax.ShapeDtypeStruct((SUBLANES, n), F32),
        grid=(n // tn,),
        in_specs=[pl.BlockSpec((SUBLANES, D), lambda j: (0, 0)),
                  pl.BlockSpec((D, tn), lambda j: (0, j)),
                  pl.BlockSpec((1, tn), lambda j: (0, j))],
        out_specs=pl.BlockSpec((SUBLANES, tn), lambda j: (0, j)),
        compiler_params=_cparams(("parallel",)),
        name="adaln",
    )(cond8, w, b)


def _fold_kernel(nope_ref, wuk_ref, wuv_ref, wo_ref, wqa_ref, wvo_ref):
    hp = lax.Precision.HIGHEST
    wqa_ref[...] = lax.dot_general(nope_ref[...], wuk_ref[...], (((1,), (1,)), ((), ())),
                                   precision=hp, preferred_element_type=F32)
    wvo_ref[...] = jnp.dot(wuv_ref[...], wo_ref[...], precision=hp, preferred_element_type=F32)


def _fold(nope, wuk, wuv, wo):
    sq = lambda *s: pl.BlockSpec((None,) + s, lambda h: (h, 0, 0))
    return pl.pallas_call(
        _fold_kernel,
        out_shape=(jax.ShapeDtypeStruct((HEADS, Q_LORA, KV_LORA), F32),
                   jax.ShapeDtypeStruct((HEADS, KV_LORA, D), F32)),
        grid=(HEADS,),
        in_specs=[sq(Q_LORA, QK_NOPE), sq(KV_LORA, QK_NOPE), sq(KV_LORA, V_HEAD), sq(V_HEAD, D)],
        out_specs=(sq(Q_LORA, KV_LORA), sq(KV_LORA, D)),
        compiler_params=_cparams(("parallel",)),
        name="mla_fold",
    )(nope, wuk, wuv, wo)


def _inproj_kernel(x_ref, mod_ref, g_ref, wa_ref, wb_ref, za_ref, zb_ref):
    h = _rms(x_ref[...], g_ref[...]) * (1.0 + mod_ref[1]) + mod_ref[0]
    hb = h.astype(BF16)
    za_ref[...] = jnp.dot(hb, wa_ref[...], preferred_element_type=F32)
    for j in range(D_ZB // D):
        zb_ref[:, j * D:(j + 1) * D] = jnp.dot(
            hb, wb_ref[:, j * D:(j + 1) * D], preferred_element_type=F32).astype(BF16)


def _inproj(x, mods, g, wa, wb, nct):
    b, nt, _ = x.shape
    return pl.pallas_call(
        _inproj_kernel,
        out_shape=(jax.ShapeDtypeStruct((b, nt, D_ZA), F32),
                   jax.ShapeDtypeStruct((b, nt, D_ZB), BF16)),
        grid=(b, nt // TM),
        in_specs=[pl.BlockSpec((None, TM, D), lambda bi, i: (bi, i, 0)),
                  pl.BlockSpec((None, None, 6, 1, D), lambda bi, i: (bi, _seg(i, nct), 0, 0, 0)),
                  pl.BlockSpec((1, D), lambda bi, i: (0, 0)),
                  pl.BlockSpec((D, D_ZA), lambda bi, i: (0, 0)),
                  pl.BlockSpec((D, D_ZB), lambda bi, i: (0, 0))],
        out_specs=(pl.BlockSpec((None, TM, D_ZA), lambda bi, i: (bi, i, 0)),
                   pl.BlockSpec((None, TM, D_ZB), lambda bi, i: (bi, i, 0))),
        compiler_params=_cparams(("parallel", "parallel"), VMEM_LIMIT),
        name="in_proj",
    )(x, mods, g, wa, wb)


def _mlaproj_kernel(za_ref, qn_ref, kvn_ref, wq_ref, cq_ref, sq_ref, ck_ref, q_ref, kv_ref, v1_ref):
    za = za_ref[...]
    cq = _rms(za[:, :Q_LORA], qn_ref[...]).astype(BF16)
    r = jnp.dot(cq, wq_ref[...], preferred_element_type=F32)
    cqt = cq_ref[...]
    sqt = sq_ref[...]
    for h in range(HEADS):
        a = r[:, h * QH:(h + 1) * QH]
        s = r[:, (HEADS + h) * QH:(HEADS + h + 1) * QH]
        q_ref[:, h * QH:(h + 1) * QH] = (a * cqt + s * sqt).astype(BF16)
    ckv = _rms(za[:, Q_LORA:Q_LORA + KV_LORA], kvn_ref[...])
    kv_ref[:, :KV_LORA] = ckv.astype(BF16)
    kv_ref[:, KV_LORA:] = (za[:, Q_LORA + KV_LORA:] * ck_ref[...]).astype(BF16)
    lane = lax.broadcasted_iota(jnp.int32, (TM, LANES), 1)
    v1_ref[:, :KV_LORA] = ckv.astype(BF16)
    v1_ref[:, KV_LORA:] = jnp.where(lane == 0, 1.0, 0.0).astype(BF16)


def _mlaproj(za, qn, kvn, wq, cq, sq, ck):
    b, nt, _ = za.shape
    return pl.pallas_call(
        _mlaproj_kernel,
        out_shape=(jax.ShapeDtypeStruct((b, nt, HEADS * QH), BF16),
                   jax.ShapeDtypeStruct((b, nt, QH), BF16),
                   jax.ShapeDtypeStruct((b, nt, 2 * LANES), BF16)),
        grid=(b, nt // TM),
        in_specs=[pl.BlockSpec((None, TM, D_ZA), lambda bi, i: (bi, i, 0)),
                  pl.BlockSpec((1, Q_LORA), lambda bi, i: (0, 0)),
                  pl.BlockSpec((1, KV_LORA), lambda bi, i: (0, 0)),
                  pl.BlockSpec((Q_LORA, 2 * HEADS * QH), lambda bi, i: (0, 0)),
                  pl.BlockSpec((TM, QH), lambda bi, i: (i, 0)),
                  pl.BlockSpec((TM, QH), lambda bi, i: (i, 0)),
                  pl.BlockSpec((TM, 2 * QK_ROPE), lambda bi, i: (i, 0))],
        out_specs=(pl.BlockSpec((None, TM, HEADS * QH), lambda bi, i: (bi, i, 0)),
                   pl.BlockSpec((None, TM, QH), lambda bi, i: (bi, i, 0)),
                   pl.BlockSpec((None, TM, 2 * LANES), lambda bi, i: (bi, i, 0))),
        compiler_params=_cparams(("parallel", "parallel"), VMEM_LIMIT),
        name="mla_proj",
    )(za, qn, kvn, wq, cq, sq, ck)


def _attn_kernel(q_ref, kv_ref, v1_ref, o_ref, q_s, m_s, acc_s, s_a, s_b, s_c,
                 *, n_ctx, n_lat, n_ctx_qsteps):
    i = pl.program_id(1)
    m_s[...] = jnp.full(m_s.shape, NEG_INF, F32)
    acc_s[...] = jnp.zeros(acc_s.shape, F32)
    for c in range(ATT_CHAINS):
        for h in range(HEADS):
            q_s[c, h * TQ:(h + 1) * TQ, :] = q_ref[c * TQ:(c + 1) * TQ, h * QH:(h + 1) * QH]

    def scores(c, s_ref, k0, size):
        k = kv_ref[pl.ds(k0, size), :]
        s_ref[c, :, :size] = lax.dot_general(q_s[c], k, (((1,), (1,)), ((), ())),
                                             preferred_element_type=F32)

    def softmax_pv(c, s_ref, k0, size):
        s = s_ref[c, :, :size]
        m_old = m_s[c]
        m_new = jnp.maximum(m_old, jnp.max(s, axis=1, keepdims=True))
        alpha = jnp.exp2(m_old - m_new)
        p = jnp.exp2(s - m_new).astype(BF16)
        acc_s[c] = alpha * acc_s[c] + jnp.dot(p, v1_ref[pl.ds(k0, size), :],
                                              preferred_element_type=F32)
        m_s[c] = m_new

    chains = range(ATT_CHAINS)

    @pl.when(i < n_ctx_qsteps)
    def _():
        for c in chains:
            scores(c, s_c, 0, n_ctx)
        for c in chains:
            softmax_pv(c, s_c, 0, n_ctx)

    tiles = [(s_c, 0, n_ctx)] + [((s_a, s_b)[t % 2], n_ctx + t * TK, TK) for t in range(n_lat // TK)]

    @pl.when(i >= n_ctx_qsteps)
    def _():
        for c in chains:
            scores(c, *tiles[0])
        for t, tile in enumerate(tiles):
            if t + 1 < len(tiles):
                for c in chains:
                    scores(c, *tiles[t + 1])
            for c in chains:
                softmax_pv(c, *tile)

    for c in chains:
        acc = acc_s[c]
        o = (acc[:, :KV_LORA] / acc[:, KV_LORA:KV_LORA + 1]).astype(BF16)
        for h in range(HEADS):
            o_ref[c * TQ:(c + 1) * TQ, h * KV_LORA:(h + 1) * KV_LORA] = o[h * TQ:(h + 1) * TQ, :]


def _attention(q, kv, v1, n_ctx):
    b, nt, _ = kv.shape
    rows = TQ * HEADS
    step = ATT_CHAINS * TQ
    n_lat = nt - n_ctx
    assert n_lat % TK == 0 and n_ctx % step == 0 and n_lat % step == 0
    kern = functools.partial(_attn_kernel, n_ctx=n_ctx, n_lat=n_lat, n_ctx_qsteps=n_ctx // step)
    return pl.pallas_call(
        kern,
        out_shape=jax.ShapeDtypeStruct((b, nt, HEADS * KV_LORA), BF16),
        grid=(b, nt // step),
        in_specs=[pl.BlockSpec((None, step, HEADS * QH), lambda bi, i: (bi, i, 0)),
                  pl.BlockSpec((None, nt, QH), lambda bi, i: (bi, 0, 0)),
                  pl.BlockSpec((None, nt, 2 * LANES), lambda bi, i: (bi, 0, 0))],
        out_specs=pl.BlockSpec((None, step, HEADS * KV_LORA), lambda bi, i: (bi, i, 0)),
        scratch_shapes=[pltpu.VMEM((ATT_CHAINS, rows, QH), BF16),
                        pltpu.VMEM((ATT_CHAINS, rows, 1), F32),
                        pltpu.VMEM((ATT_CHAINS, rows, 2 * LANES), F32),
                        pltpu.VMEM((ATT_CHAINS, rows, TK), F32),
                        pltpu.VMEM((ATT_CHAINS, rows, TK), F32),
                        pltpu.VMEM((ATT_CHAINS, rows, n_ctx), F32)],
        compiler_params=_cparams(("parallel", "arbitrary"), VMEM_LIMIT),
        name="mla_attention",
    )(q, kv, v1)


def _lru_tile(j, nct, ntl, reverse):
    if not reverse:
        return j
    return jnp.where(j < nct, nct - 1 - j, ntl - 1 - (j - nct))


def _lru_kernel(x_ref, prev_ref, next_ref, cw_ref, cb_ref, w_ref, ba_ref, bx_ref, lam_ref,
                o_ref, a_s, u_s, h_s, *, nct, ntl, reverse):
    j = pl.program_id(1)
    tile = _lru_tile(j, nct, ntl, reverse)

    @pl.when(j == 0)
    def _():
        h_s[...] = jnp.zeros(h_s.shape, F32)

    x = x_ref[...].astype(F32)
    has_prev = jnp.logical_and(tile != 0, tile != nct)
    has_next = jnp.logical_and(tile != nct - 1, tile != ntl - 1)
    prev = jnp.where(has_prev, prev_ref[...].astype(F32), 0.0)
    nxt = jnp.where(has_next, next_ref[...].astype(F32), 0.0)
    row = lax.broadcasted_iota(jnp.int32, (TM, D), 0)
    xm1 = jnp.where(row == 0, prev[HALO - 1:HALO, :], pltpu.roll(x, 1, 0))
    xm2 = jnp.where(row == 0, prev[HALO - 2:HALO - 1, :],
                    jnp.where(row == 1, prev[HALO - 1:HALO, :], pltpu.roll(x, 2, 0)))
    xp1 = jnp.where(row == TM - 1, nxt[0:1, :], pltpu.roll(x, TM - 1, 0))
    cw = cw_ref[...]
    xl = cb_ref[...] + xm2 * cw[0:1, :] + xm1 * cw[1:2, :] + x * cw[2:3, :] + xp1 * cw[3:4, :]

    z = -lam_ref[...]
    softplus = jnp.maximum(z, 0.0) + jnp.log1p(jnp.exp(-jnp.abs(z)))
    rate = -LRU_C * softplus
    for g in range(D // LANES):
        sl = slice(g * LANES, (g + 1) * LANES)
        xg = xl[:, sl]
        pre = jnp.dot(xg.astype(BF16), w_ref[g], preferred_element_type=F32)
        r = _sigmoid(pre[:, :LANES] + ba_ref[:, sl])
        ig = _sigmoid(pre[:, LANES:] + bx_ref[:, sl])
        log_a = rate[:, sl] * r
        a = jnp.exp(log_a)
        a_s[:, sl] = a
        u_s[:, sl] = jnp.sqrt(-jnp.tanh(log_a) * (1.0 + a * a)) * (ig * xg)

    srow = lax.broadcasted_iota(jnp.int32, (SUBLANES, D), 0)
    n_chunks = TM // SUBLANES

    def chunk(c, h_prev):
        cc = (n_chunks - 1 - c) if reverse else c
        r0 = pl.multiple_of(cc * SUBLANES, SUBLANES)
        a = a_s[pl.ds(r0, SUBLANES), :]
        u = u_s[pl.ds(r0, SUBLANES), :]
        for s in (1, 2, 4):
            if reverse:
                keep = srow < SUBLANES - s
                shift = SUBLANES - s
            else:
                keep = srow >= s
                shift = s
            u = u + a * jnp.where(keep, pltpu.roll(u, shift, 0), 0.0)
            a = a * jnp.where(keep, pltpu.roll(a, shift, 0), 1.0)
        h = u + a * h_prev
        o_ref[pl.ds(r0, SUBLANES), :] = h
        edge = h[0:1, :] if reverse else h[SUBLANES - 1:SUBLANES, :]
        return jnp.broadcast_to(edge, (SUBLANES, D))

    h_s[...] = lax.fori_loop(0, n_chunks, chunk, h_s[...])


def _lru(zb, cw, cb, wd, ba, bx, lam, nct, reverse):
    b, nt, _ = zb.shape
    ntl = nt // TM
    hb = TM // HALO
    n_hb = nt // HALO
    tile = lambda j: _lru_tile(j, nct, ntl, reverse)
    kern = functools.partial(_lru_kernel, nct=nct, ntl=ntl, reverse=reverse)
    vec = lambda n: pl.BlockSpec((n, D), lambda bi, j: (0, 0))
    return pl.pallas_call(
        kern,
        out_shape=jax.ShapeDtypeStruct((b, nt, D), F32),
        grid=(b, ntl),
        in_specs=[pl.BlockSpec((None, TM, D), lambda bi, j: (bi, tile(j), 0)),
                  pl.BlockSpec((None, HALO, D),
                               lambda bi, j: (bi, jnp.maximum(tile(j) * hb - 1, 0), 0)),
                  pl.BlockSpec((None, HALO, D),
                               lambda bi, j: (bi, jnp.minimum((tile(j) + 1) * hb, n_hb - 1), 0)),
                  vec(CONV_W), vec(1),
                  pl.BlockSpec((D // LANES, LANES, 2 * LANES), lambda bi, j: (0, 0, 0)),
                  vec(1), vec(1), vec(1)],
        out_specs=pl.BlockSpec((None, TM, D), lambda bi, j: (bi, tile(j), 0)),
        scratch_shapes=[pltpu.VMEM((TM, D), F32), pltpu.VMEM((TM, D), F32),
                        pltpu.VMEM((SUBLANES, D), F32)],
        compiler_params=_cparams(("arbitrary", "arbitrary"), VMEM_LIMIT),
        name="rglru_bwd" if reverse else "rglru_fwd",
    )(zb, zb, zb, cw, cb, wd, ba, bx, lam)


def _merge_kernel(o_ref, hf_ref, hb_ref, gr_ref, ga_ref, gl_ref, x_ref, mod_ref, bg_ref, nf_ref,
                  wvo_ref, wol_ref, wout_ref, wr_ref, br_ref,
                  xn_ref, h2_ref, rg_ref, rt_ref, meta_ref, cnt_ref, carry_s):
    first = jnp.logical_and(pl.program_id(0) == 0, pl.program_id(1) == 0)

    @pl.when(first)
    def _():
        carry_s[...] = jnp.zeros(carry_s.shape, F32)

    gr = gr_ref[...].astype(F32)
    gelu = 0.5 * gr * (1.0 + jnp.tanh(math.sqrt(2.0 / math.pi) * (gr + 0.044715 * (gr * gr * gr))))
    lru = ((hf_ref[...] + hb_ref[...]) * gelu).astype(BF16)
    y_att = jnp.dot(o_ref[...], wvo_ref[...], preferred_element_type=F32)
    y_lru = jnp.dot(lru, wol_ref[...], preferred_element_type=F32)
    m = (_sigmoid(ga_ref[...].astype(F32) + bg_ref[0]) * y_att
         + _sigmoid(gl_ref[...].astype(F32) + bg_ref[1]) * y_lru)
    out = jnp.dot(m.astype(BF16), wout_ref[...], preferred_element_type=F32)
    xn = x_ref[...] + mod_ref[2] * out
    xn_ref[...] = xn
    h2 = _rms(xn, nf_ref[...]) * (1.0 + mod_ref[4]) + mod_ref[3]
    h2_ref[...] = h2.astype(BF16)

    wr = wr_ref[...]
    h_hi = h2.astype(BF16)
    h_lo = (h2 - h_hi.astype(F32)).astype(BF16)
    w_hi = wr.astype(BF16)
    w_lo = (wr - w_hi.astype(F32)).astype(BF16)
    logits = (jnp.dot(h_hi, w_hi, preferred_element_type=F32)
              + (jnp.dot(h_lo, w_hi, preferred_element_type=F32)
                 + jnp.dot(h_hi, w_lo, preferred_element_type=F32))) + br_ref[...]
    lane = lax.broadcasted_iota(jnp.int32, (TM, LANES), 1)
    lane_f = lane.astype(F32)
    l = logits
    vals, idxs = [], []
    for _ in range(TOP_K):
        mx = jnp.max(l, axis=1, keepdims=True)
        ix = jnp.min(jnp.where(l == mx, lane_f, float(LANES)), axis=1, keepdims=True)
        vals.append(mx)
        idxs.append(ix)
        l = jnp.where(lane_f == ix, NEG_INF, l)
    chosen = jnp.where(jnp.logical_and(l == NEG_INF, lane < N_EXPERTS), 1.0, 0.0)
    es = [jnp.exp(v - vals[0]) for v in vals]
    den = es[0] + es[1] + es[2] + es[3]

    r_i = lax.broadcasted_iota(jnp.int32, (TM, TM), 0)
    c_i = lax.broadcasted_iota(jnp.int32, (TM, TM), 1)
    tri = jnp.where(r_i > c_i, 1.0, 0.0).astype(BF16)
    before = jnp.dot(tri, chosen.astype(BF16), preferred_element_type=F32)
    cnt_t = jnp.sum(chosen, axis=0, keepdims=True)
    seg_len = jnp.floor((cnt_t + (MOE_CH - 1.0)) * (1.0 / MOE_CH)) * MOE_CH
    e_r = lax.broadcasted_iota(jnp.int32, (LANES, LANES), 0)
    e_c = lax.broadcasted_iota(jnp.int32, (LANES, LANES), 1)
    upper = jnp.where(e_r < e_c, 1.0, 0.0).astype(BF16)
    seg_off = jnp.dot(jnp.broadcast_to(seg_len, (SUBLANES, LANES)).astype(BF16), upper,
                      preferred_element_type=F32)[0:1, :]
    slot_of = before + seg_off
    rg = jnp.zeros((TM, LANES), F32)
    for k in range(TOP_K):
        slot = jnp.sum(jnp.where(lane_f == idxs[k], slot_of, 0.0), axis=1, keepdims=True)
        rg = jnp.where(lane == k, es[k] / den, rg)
        rg = jnp.where(lane == TOP_K + k, slot, rg)
    rg_ref[...] = rg
    rt_ref[...] = jnp.transpose(rg)[0:SUBLANES, :]
    row8 = lax.broadcasted_iota(jnp.int32, (SUBLANES, LANES), 0)
    carry = carry_s[...]
    meta_ref[...] = jnp.where(row8 == 0, cnt_t, jnp.where(row8 == 1, carry, 0.0)).astype(jnp.int32)
    carry = carry + jnp.floor((cnt_t + (SUBLANES - 1.0)) * (1.0 / SUBLANES)) * SUBLANES
    carry_s[...] = carry
    cnt_ref[...] = jnp.broadcast_to(carry, cnt_ref.shape).astype(jnp.int32)


def _merge(o, hf, hb, zb, x, mods, bg, nf, wvo, wol, wout, wr, br, nct, tile0):
    b, nt, _ = x.shape
    ntl = nt // TM
    tok = lambda c: pl.BlockSpec((None, TM, D), lambda bi, j: (bi, j + tile0, c))
    full = lambda *s: pl.BlockSpec(s, lambda bi, j: (0,) * len(s))
    out_nt = nt - tile0 * TM
    return pl.pallas_call(
        _merge_kernel,
        out_shape=(jax.ShapeDtypeStruct((b, out_nt, D), F32),
                   jax.ShapeDtypeStruct((b, out_nt, D), BF16),
                   jax.ShapeDtypeStruct((b, out_nt, LANES), F32),
                   jax.ShapeDtypeStruct((b, out_nt // TM, SUBLANES, TM), F32),
                   jax.ShapeDtypeStruct((b, out_nt // TM, SUBLANES, LANES), jnp.int32),
                   jax.ShapeDtypeStruct((SUBLANES, LANES), jnp.int32)),
        grid=(b, ntl - tile0),
        in_specs=[tok(0), tok(0), tok(0), tok(1), tok(2), tok(3), tok(0),
                  pl.BlockSpec((None, None, 6, 1, D),
                               lambda bi, j: (bi, _seg(j + tile0, nct), 0, 0, 0)),
                  full(2, 1, D), full(1, D), full(D, D), full(D, D), full(D, D),
                  full(D, LANES), full(1, LANES)],
        out_specs=(pl.BlockSpec((None, TM, D), lambda bi, j: (bi, j, 0)),
                   pl.BlockSpec((None, TM, D), lambda bi, j: (bi, j, 0)),
                   pl.BlockSpec((None, TM, LANES), lambda bi, j: (bi, j, 0)),
                   pl.BlockSpec((None, None, SUBLANES, TM), lambda bi, j: (bi, j, 0, 0)),
                   pl.BlockSpec((None, None, SUBLANES, LANES), lambda bi, j: (bi, j, 0, 0)),
                   pl.BlockSpec((SUBLANES, LANES), lambda bi, j: (0, 0))),
        scratch_shapes=[pltpu.VMEM((1, LANES), F32)],
        compiler_params=_cparams(("arbitrary", "arbitrary"), VMEM_LIMIT),
        name="merge_route",
    )(o, hf, hb, zb, zb, zb, x, mods, bg, nf, wvo, wol, wout, wr, br)


def _dispatch_kernel(meta_ref, h2_ref, rt_ref, xs_ref, sorted_s, zero_s, sem, *, n_blk, ntl):
    bi = pl.program_id(0)
    j = pl.program_id(1)
    first = jnp.logical_and(bi == 0, j == 0)

    @pl.when(first)
    def _():
        zero_s[...] = jnp.zeros(zero_s.shape, F32)

        def pad_copy(i):
            r0 = pl.multiple_of(i * SUBLANES, SUBLANES)
            return pltpu.make_async_copy(zero_s.at[pl.ds(0, SUBLANES), :],
                                         xs_ref.at[pl.ds(r0, SUBLANES), :], sem)

        def blk_copy(i):
            r0 = pl.multiple_of(i * MOE_R, MOE_R)
            return pltpu.make_async_copy(zero_s, xs_ref.at[pl.ds(r0, MOE_R), :], sem)

        def blk_issue(i, c):
            blk_copy(i).start()
            return c

        def blk_drain(i, c):
            blk_copy(i).wait()
            return c
        n_used = meta_ref[META_NUSED]
        lax.fori_loop(n_used, n_blk, blk_issue, 0)
        lax.fori_loop(n_used, n_blk, blk_drain, 0)

        def per_expert(e, c):
            lo = meta_ref[META_TAIL + e] // SUBLANES
            hi = meta_ref[META_END + e] // SUBLANES

            def issue(r, c2):
                pad_copy(r).start()
                return c2

            def drain(r, c2):
                pad_copy(r).wait()
                return c2
            lax.fori_loop(lo, hi, issue, 0)
            lax.fori_loop(lo, hi, drain, 0)
            return c
        lax.fori_loop(0, N_EXPERTS, per_expert, 0)

    slot_i = lax.broadcasted_iota(jnp.int32, (MOE_SLOTS, TM), 0).astype(F32)
    hit = slot_i == rt_ref[TOP_K:TOP_K + 1, :]
    for k in range(1, TOP_K):
        hit = jnp.logical_or(hit, slot_i == rt_ref[TOP_K + k:TOP_K + k + 1, :])
    onehot = jnp.where(hit, 1.0, 0.0).astype(BF16)
    sorted_s[...] = jnp.dot(onehot, h2_ref[...], preferred_element_type=F32)

    _moe_chunk_copies(meta_ref, bi * ntl + j, sem,
                      lambda slot, row: (sorted_s.at[pl.ds(slot, MOE_CH), :],
                                         xs_ref.at[pl.ds(row, MOE_CH), :]))


def _moe_chunk_copies(meta_ref, tile, sem, refs):
    base = META_TILES + tile * (2 * N_EXPERTS)

    def sweep(start):
        def per_expert(e, slot0):
            n = meta_ref[base + e]
            row0 = meta_ref[e] + meta_ref[base + N_EXPERTS + e]
            n_ch = (n + MOE_CH - 1) // MOE_CH

            def piece(c, carry):
                src, dst = refs(pl.multiple_of(slot0 + c * MOE_CH, MOE_CH),
                                pl.multiple_of(row0 + c * MOE_CH, SUBLANES))
                cp = pltpu.make_async_copy(src, dst, sem)
                if start:
                    cp.start()
                else:
                    cp.wait()
                return carry
            lax.fori_loop(0, n_ch, piece, 0)
            return slot0 + n_ch * MOE_CH
        lax.fori_loop(0, N_EXPERTS, per_expert, 0)
    sweep(True)
    sweep(False)


def _dispatch(meta, h2, rt, n_rows):
    b, nt, _ = h2.shape
    ntl = nt // TM
    kern = functools.partial(_dispatch_kernel, n_blk=n_rows // MOE_R, ntl=ntl)
    return pl.pallas_call(
        kern,
        out_shape=jax.ShapeDtypeStruct((n_rows, D), F32),
        grid_spec=pltpu.PrefetchScalarGridSpec(
            num_scalar_prefetch=1,
            grid=(b, ntl),
            in_specs=[pl.BlockSpec((None, TM, D), lambda bi, j, meta: (bi, j, 0)),
                      pl.BlockSpec((None, None, SUBLANES, TM), lambda bi, j, meta: (bi, j, 0, 0))],
            out_specs=pl.BlockSpec(memory_space=pl.ANY),
            scratch_shapes=[pltpu.VMEM((MOE_SLOTS, D), F32),
                            pltpu.VMEM((MOE_R, D), F32),
                            pltpu.SemaphoreType.DMA]),
        compiler_params=_cparams(("arbitrary", "arbitrary"), VMEM_LIMIT),
        name="moe_dispatch",
    )(meta, h2, rt)


def _expert_kernel(be_ref, nu_ref, x_ref, wg_ref, bg_ref, wu_ref, bu_ref, wd_ref, bd_ref,
                   y_ref, wg_s, wu_s, wd_s):
    blk = pl.program_id(0)
    used = blk < nu_ref[0]
    prev = be_ref[jnp.maximum(blk - 1, 0)]
    fresh = jnp.logical_or(blk == 0, be_ref[blk] != prev)

    @pl.when(jnp.logical_and(used, fresh))
    def _():
        wg_s[...] = wg_ref[...].astype(BF16)
        wu_s[...] = wu_ref[...].astype(BF16)
        wd_s[...] = wd_ref[...].astype(BF16)

    @pl.when(used)
    def _():
        x = x_ref[...].astype(BF16)
        g = jnp.dot(x, wg_s[...], preferred_element_type=F32) + bg_ref[...]
        u = jnp.dot(x, wu_s[...], preferred_element_type=F32) + bu_ref[...]
        g = jnp.minimum(g, SWIGLU_LIMIT)
        u = jnp.clip(u, -SWIGLU_LIMIT, SWIGLU_LIMIT)
        act = g * _sigmoid(SWIGLU_ALPHA * g) * (u + 1.0)
        y_ref[...] = jnp.dot(act.astype(BF16), wd_s[...], preferred_element_type=F32) + bd_ref[...]

    @pl.when(jnp.logical_not(used))
    def _():
        y_ref[...] = jnp.zeros(y_ref.shape, F32)


def _experts(blk_expert, n_used, xs, wg, bg, wu, bu, wd, bd, layer):
    n_rows = xs.shape[0]
    n_blk = n_rows // MOE_R
    dff = wg.shape[-1]
    e_of = lambda i, be, nu: be[jnp.minimum(i, nu[0] - 1)]
    wspec = lambda r, c: pl.BlockSpec((None, None, r, c), lambda i, be, nu: (layer, e_of(i, be, nu), 0, 0))
    return pl.pallas_call(
        _expert_kernel,
        out_shape=jax.ShapeDtypeStruct((n_rows, D), F32),
        grid_spec=pltpu.PrefetchScalarGridSpec(
            num_scalar_prefetch=2,
            grid=(n_blk,),
            in_specs=[pl.BlockSpec((MOE_R, D), lambda i, be, nu: (jnp.minimum(i, nu[0] - 1), 0)),
                      wspec(D, dff), wspec(1, dff), wspec(D, dff), wspec(1, dff),
                      wspec(dff, D), wspec(1, D)],
            out_specs=pl.BlockSpec((MOE_R, D), lambda i, be, nu: (i, 0)),
            scratch_shapes=[pltpu.VMEM((D, dff), BF16), pltpu.VMEM((D, dff), BF16),
                            pltpu.VMEM((dff, D), BF16)]),
        compiler_params=_cparams(("arbitrary",), VMEM_LIMIT),
        name="moe_experts",
    )(blk_expert, n_used, xs, wg, bg, wu, bu, wd, bd)


def _combine_kernel(meta_ref, x_ref, rg_ref, mod_ref, fn_ref, ys_ref, o_ref, buf_s, sem, *, final, ntl):
    bi = pl.program_id(0)
    j = pl.program_id(1)

    @pl.when(jnp.logical_and(bi == 0, j == 0))
    def _():
        buf_s[...] = jnp.zeros(buf_s.shape, F32)

    _moe_chunk_copies(meta_ref, bi * ntl + j, sem,
                      lambda slot, row: (ys_ref.at[pl.ds(row, MOE_CH), :],
                                         buf_s.at[pl.ds(slot, MOE_CH), :]))

    rg = rg_ref[...]
    slot_i = lax.broadcasted_iota(jnp.int32, (TM, MOE_SLOTS), 1).astype(F32)
    gate = jnp.zeros((TM, MOE_SLOTS), F32)
    for k in range(TOP_K):
        gate = jnp.where(slot_i == rg[:, TOP_K + k:TOP_K + k + 1], rg[:, k:k + 1], gate)
    y = jnp.dot(gate.astype(BF16), buf_s[...].astype(BF16), preferred_element_type=F32)
    xo = x_ref[...] + mod_ref[5] * y
    if final:
        xo = _rms(xo, fn_ref[...])
    o_ref[...] = xo


def _combine(meta, x, rg, mods, fn, ys, nct, final):
    b, nt, _ = x.shape
    ntl = nt // TM
    kern = functools.partial(_combine_kernel, final=final, ntl=ntl)
    return pl.pallas_call(
        kern,
        out_shape=jax.ShapeDtypeStruct((b, nt, D), F32),
        grid_spec=pltpu.PrefetchScalarGridSpec(
            num_scalar_prefetch=1,
            grid=(b, ntl),
            in_specs=[pl.BlockSpec((None, TM, D), lambda bi, j, meta: (bi, j, 0)),
                      pl.BlockSpec((None, TM, LANES), lambda bi, j, meta: (bi, j, 0)),
                      pl.BlockSpec((None, None, 6, 1, D),
                                   lambda bi, j, meta: (bi, _seg(j, nct), 0, 0, 0)),
                      pl.BlockSpec((1, D), lambda bi, j, meta: (0, 0)),
                      pl.BlockSpec(memory_space=pl.ANY)],
            out_specs=pl.BlockSpec((None, TM, D), lambda bi, j, meta: (bi, j, 0)),
            scratch_shapes=[pltpu.VMEM((MOE_SLOTS, D), F32), pltpu.SemaphoreType.DMA]),
        compiler_params=_cparams(("arbitrary", "arbitrary"), VMEM_LIMIT),
        name="moe_combine",
    )(meta, x, rg, mods, fn, ys)


_ROT_PARTNER = tuple(list(range(16, 32)) + list(range(0, 16)) + list(range(48, 64)) + list(range(32, 48)))


def _rope_tables(n_ctx, n_lat):
    n = jnp.arange(n_lat)
    row = (n // GRID_W).astype(F32)
    col = (n % GRID_W).astype(F32)
    n_freq = QK_ROPE // 4
    inv = ROPE_THETA ** (-jnp.arange(n_freq, dtype=F32) / n_freq)
    ar = row[:, None] * inv
    ac = col[:, None] * inv
    c64 = jnp.concatenate([jnp.cos(ar), jnp.cos(ar), jnp.cos(ac), jnp.cos(ac)], axis=1)
    s64 = jnp.concatenate([-jnp.sin(ar), jnp.sin(ar), -jnp.sin(ac), jnp.sin(ac)], axis=1)
    c64 = jnp.concatenate([jnp.ones((n_ctx, QK_ROPE), F32), c64], axis=0)
    s64 = jnp.concatenate([jnp.zeros((n_ctx, QK_ROPE), F32), s64], axis=0)
    nt = n_ctx + n_lat
    qs = ATTN_SCALE * LOG2E
    cq = qs * jnp.concatenate([jnp.ones((nt, QK_NOPE), F32), c64, c64], axis=1)
    sq = qs * jnp.concatenate([jnp.zeros((nt, QK_NOPE), F32), s64, s64], axis=1)
    ck = jnp.concatenate([c64, s64], axis=1)
    return cq, sq, ck


def _block_diag_pairs(w):
    w = w.reshape(LRU_BLOCKS // 2, 2, LRU_BW, LRU_BW)
    z = jnp.zeros((LRU_BLOCKS // 2, LRU_BW, LRU_BW), w.dtype)
    top = jnp.concatenate([w[:, 0], z], axis=2)
    bot = jnp.concatenate([z, w[:, 1]], axis=2)
    return jnp.concatenate([top, bot], axis=1)


def kernel(x, c, ctx, c_ctx, w_ada, b_ada, norm_mix, norm_ffn, w_in, b_branch_gate, q_norm, w_uq, kv_norm, w_ukv, w_o_attn, conv_w, conv_b, lru_w_a, lru_b_a, lru_w_x, lru_b_x, lru_lambda, w_o_lru, w_out, w_router, b_router, w_exp_gate, b_exp_gate, w_exp_up, b_exp_up, w_exp_down, b_exp_down, final_norm):
    depth = w_in.shape[0]
    b, n_lat, _ = x.shape
    n_ctx = ctx.shape[1]
    assert n_ctx % TM == 0 and n_lat % TK == 0 and b + 1 <= SUBLANES
    nct = n_ctx // TM
    nt = n_ctx + n_lat
    ntl = nt // TM
    perm = jnp.array(_ROT_PARTNER)

    xs = jnp.concatenate([ctx, x], axis=1)
    cond8 = jnp.zeros((SUBLANES, D), F32).at[:b].set(c).at[b].set(c_ctx)
    cq_t, sq_t, ck_t = _rope_tables(n_ctx, n_lat)

    for l in range(depth):
        last = l == depth - 1
        tile0 = nct if last else 0

        m8 = _ada(cond8, w_ada[l], b_ada[l][None, :]).reshape(SUBLANES, 6, 1, D)
        mods = jnp.stack([jnp.broadcast_to(m8[b], (b,) + m8.shape[1:]), m8[:b]], axis=1)

        wi = w_in[l]
        kpe_w = wi[:, Q_LORA + KV_LORA:Q_LORA + KV_LORA + QK_ROPE]
        wa = jnp.concatenate([wi[:, :Q_LORA + KV_LORA + QK_ROPE], kpe_w[:, perm]], axis=1).astype(BF16)
        wb = wi[:, Q_LORA + KV_LORA + QK_ROPE:].astype(BF16)
        uq = w_uq[l].reshape(Q_LORA, HEADS, QK_DIM).transpose(1, 0, 2)
        ukv = w_ukv[l].reshape(KV_LORA, HEADS, QK_NOPE + V_HEAD).transpose(1, 0, 2)
        wqa, wvo = _fold(uq[:, :, :QK_NOPE], ukv[:, :, :QK_NOPE], ukv[:, :, QK_NOPE:],
                         w_o_attn[l].reshape(HEADS, V_HEAD, D))
        pe = uq[:, :, QK_NOPE:]
        pe_sw = pe[:, :, perm]
        zq = jnp.zeros((HEADS, Q_LORA, QK_NOPE), F32)
        wq1 = jnp.concatenate([wqa, pe, pe], axis=2)
        wq2 = jnp.concatenate([zq, pe_sw, pe_sw], axis=2)
        wq = jnp.concatenate([wq1.transpose(1, 0, 2).reshape(Q_LORA, HEADS * QH),
                              wq2.transpose(1, 0, 2).reshape(Q_LORA, HEADS * QH)], axis=1).astype(BF16)
        wvo = wvo.reshape(D, D).astype(BF16)
        wr = jnp.zeros((D, LANES), F32).at[:, :N_EXPERTS].set(w_router[l])
        br = jnp.full((1, LANES), NEG_INF, F32).at[0, :N_EXPERTS].set(b_router[l])

        za, zb = _inproj(xs, mods, norm_mix[l][None, :], wa, wb, nct)
        q, kv, v1 = _mlaproj(za, q_norm[l][None, :], kv_norm[l][None, :], wq, cq_t, sq_t, ck_t)
        o = _attention(q, kv, v1, n_ctx)
        hs = []
        for d in range(2):
            wd = jnp.concatenate([_block_diag_pairs(lru_w_a[l, d]), _block_diag_pairs(lru_w_x[l, d])],
                                 axis=2).astype(BF16)
            hs.append(_lru(zb, conv_w[l], conv_b[l][None, :], wd, lru_b_a[l, d][None, :],
                           lru_b_x[l, d][None, :], lru_lambda[l, d][None, :], nct, reverse=d == 1))
        xn, h2, rg, rt, tile_meta, cnt = _merge(
            o, hs[0], hs[1], zb, xs, mods, b_branch_gate[l].reshape(2, 1, D), norm_ffn[l][None, :],
            wvo, w_o_lru[l].astype(BF16), w_out[l].astype(BF16), wr, br, nct, tile0)

        counts = cnt[0, :N_EXPERTS]
        padded = ((counts + (MOE_CH - 1) + MOE_R - 1) // MOE_R) * MOE_R
        pad_end = jnp.cumsum(padded)
        pad_start = pad_end - padded
        moe_nt = xn.shape[1]
        n_tiles = b * moe_nt // TM
        max_rows = (b * moe_nt * TOP_K + n_tiles * N_EXPERTS * (SUBLANES - 1)
                    + N_EXPERTS * (MOE_CH - 1))
        n_blk = max_rows // MOE_R + N_EXPERTS
        n_rows = n_blk * MOE_R
        blk_start = jnp.arange(n_blk, dtype=jnp.int32) * MOE_R
        blk_expert = jnp.minimum(jnp.sum(blk_start[:, None] >= pad_end[None, :], axis=1),
                                 N_EXPERTS - 1).astype(jnp.int32)
        n_used = (pad_end[-1:] // MOE_R).astype(jnp.int32)
        meta = jnp.concatenate([pad_start, pad_start + counts, pad_end,
                                jnp.broadcast_to(n_used, (N_EXPERTS,)),
                                tile_meta[:, :, 0:2, :N_EXPERTS].reshape(-1)]).astype(jnp.int32)

        xsort = _dispatch(meta, h2, rt, n_rows)
        ys = _experts(blk_expert, n_used, xsort, w_exp_gate, b_exp_gate[:, :, None, :],
                      w_exp_up, b_exp_up[:, :, None, :], w_exp_down, b_exp_down[:, :, None, :], l)
        xs = _combine(meta, xn, rg, mods, final_norm[None, :], ys, nct - tile0, last)
    return xs
```

```python
import functools
import math

import jax
import jax.numpy as jnp
from jax import lax
from jax.experimental import pallas as pl
from jax.experimental.pallas import tpu as pltpu

D = 1024
GRID_W = 64
HEADS = 8
QK_NOPE = 128
QK_ROPE = 64
QK_DIM = QK_NOPE + QK_ROPE
V_HEAD = 128
Q_LORA = 256
KV_LORA = 128
ROPE_THETA = 10000.0
ATTN_SCALE = QK_DIM ** -0.5
LRU_BLOCKS = 16
LRU_BW = D // LRU_BLOCKS
CONV_W = 4
LRU_C = 8.0
N_EXPERTS = 32
TOP_K = 4
SWIGLU_ALPHA = 1.702
SWIGLU_LIMIT = 7.0
NORM_EPS = 1e-6
D_ZA = Q_LORA + KV_LORA + 2 * QK_ROPE
D_ZB = 4 * D
QH = 256

LANES = 128
SUBLANES = 8
TM = 256
HALO = 16
TQ = 128
ATT_CHAINS = 1
TK = 1024
MOE_R = 256
MOE_CH = 16
MOE_SLOTS = TM * TOP_K + N_EXPERTS * MOE_CH
META_TAIL = N_EXPERTS
META_END = 2 * N_EXPERTS
META_NUSED = 3 * N_EXPERTS
META_TILES = 4 * N_EXPERTS
VMEM_LIMIT = 56 * 1024 * 1024

F32 = jnp.float32
BF16 = jnp.bfloat16
NEG_INF = float("-inf")
LOG2E = 1.4426950408889634


def _cparams(sem, vmem=None):
    return pltpu.CompilerParams(dimension_semantics=sem, vmem_limit_bytes=vmem)


def _rms(x, g):
    return x * lax.rsqrt(jnp.mean(x * x, axis=-1, keepdims=True) + NORM_EPS) * g


def _sigmoid(x):
    return 1.0 / (1.0 + jnp.exp(-x))


def _seg(i, nct):
    return jnp.where(i < nct, 0, 1)


def _ada_kernel(c_ref, w_ref, b_ref, o_ref):
    c = c_ref[...]
    s = (c * _sigmoid(c)).astype(BF16)
    o_ref[...] = jnp.dot(s, w_ref[...].astype(BF16), preferred_element_type=F32) + b_ref[...]


def _ada(cond8, w, b, tn=768):
    n = w.shape[1]
    return pl.pallas_call(
        _ada_kernel,
        out_shape=jax.ShapeDtypeStruct((SUBLANES, n), F32),
        grid=(n // tn,),
        in_specs=[pl.BlockSpec((SUBLANES, D), lambda j: (0, 0)),
                  pl.BlockSpec((D, tn), lambda j: (0, j)),
                  pl.BlockSpec((1, tn), lambda j: (0, j))],
        out_specs=pl.BlockSpec((SUBLANES, tn), lambda j: (0, j)),
        compiler_params=_cparams(("parallel",)),
        name="adaln",
    )(cond8, w, b)


def _fold_kernel(nope_ref, wuk_ref, wuv_ref, wo_ref, wqa_ref, wvo_ref):
    hp = lax.Precision.HIGHEST
    wqa_ref[...] = lax.dot_general(nope_ref[...], wuk_ref[...], (((1,), (1,)), ((), ())),
                                   precision=hp, preferred_element_type=F32)
    wvo_ref[...] = jnp.dot(wuv_ref[...], wo_ref[...], precision=hp, preferred_element_type=F32)


def _fold(nope, wuk, wuv, wo):
    sq = lambda *s: pl.BlockSpec((None,) + s, lambda h: (h, 0, 0))
    return pl.pallas_call(
        _fold_kernel,
        out_shape=(jax.ShapeDtypeStruct((HEADS, Q_LORA, KV_LORA), F32),
                   jax.ShapeDtypeStruct((HEADS, KV_LORA, D), F32)),
        grid=(HEADS,),
        in_specs=[sq(Q_LORA, QK_NOPE), sq(KV_LORA, QK_NOPE), sq(KV_LORA, V_HEAD), sq(V_HEAD, D)],
        out_specs=(sq(Q_LORA, KV_LORA), sq(KV_LORA, D)),
        compiler_params=_cparams(("parallel",)),
        name="mla_fold",
    )(nope, wuk, wuv, wo)


def _inproj_kernel(x_ref, mod_ref, g_ref, wa_ref, wb_ref, za_ref, zb_ref):
    h = _rms(x_ref[...], g_ref[...]) * (1.0 + mod_ref[1]) + mod_ref[0]
    hb = h.astype(BF16)
    za_ref[...] = jnp.dot(hb, wa_ref[...], preferred_element_type=F32)
    for j in range(D_ZB // D):
        zb_ref[:, j * D:(j + 1) * D] = jnp.dot(
            hb, wb_ref[:, j * D:(j + 1) * D], preferred_element_type=F32).astype(BF16)


def _inproj(x, mods, g, wa, wb, nct):
    b, nt, _ = x.shape
    return pl.pallas_call(
        _inproj_kernel,
        out_shape=(jax.ShapeDtypeStruct((b, nt, D_ZA), F32),
                   jax.ShapeDtypeStruct((b, nt, D_ZB), BF16)),
        grid=(b, nt // TM),
        in_specs=[pl.BlockSpec((None, TM, D), lambda bi, i: (bi, i, 0)),
                  pl.BlockSpec((None, None, 6, 1, D), lambda bi, i: (bi, _seg(i, nct), 0, 0, 0)),
                  pl.BlockSpec((1, D), lambda bi, i: (0, 0)),
                  pl.BlockSpec((D, D_ZA), lambda bi, i: (0, 0)),
                  pl.BlockSpec((D, D_ZB), lambda bi, i: (0, 0))],
        out_specs=(pl.BlockSpec((None, TM, D_ZA), lambda bi, i: (bi, i, 0)),
                   pl.BlockSpec((None, TM, D_ZB), lambda bi, i: (bi, i, 0))),
        compiler_params=_cparams(("parallel", "parallel"), VMEM_LIMIT),
        name="in_proj",
    )(x, mods, g, wa, wb)


def _mlaproj_kernel(za_ref, qn_ref, kvn_ref, wq_ref, cq_ref, sq_ref, ck_ref, q_ref, kv_ref, v1_ref):
    za = za_ref[...]
    cq = _rms(za[:, :Q_LORA], qn_ref[...]).astype(BF16)
    r = jnp.dot(cq, wq_ref[...], preferred_element_type=F32)
    cqt = cq_ref[...]
    sqt = sq_ref[...]
    for h in range(HEADS):
        a = r[:, h * QH:(h + 1) * QH]
        s = r[:, (HEADS + h) * QH:(HEADS + h + 1) * QH]
        q_ref[:, h * QH:(h + 1) * QH] = (a * cqt + s * sqt).astype(BF16)
    ckv = _rms(za[:, Q_LORA:Q_LORA + KV_LORA], kvn_ref[...])
    kv_ref[:, :KV_LORA] = ckv.astype(BF16)
    kv_ref[:, KV_LORA:] = (za[:, Q_LORA + KV_LORA:] * ck_ref[...]).astype(BF16)
    lane = lax.broadcasted_iota(jnp.int32, (TM, LANES), 1)
    v1_ref[:, :KV_LORA] = ckv.astype(BF16)
    v1_ref[:, KV_LORA:] = jnp.where(lane == 0, 1.0, 0.0).astype(BF16)


def _mlaproj(za, qn, kvn, wq, cq, sq, ck):
    b, nt, _ = za.shape
    return pl.pallas_call(
        _mlaproj_kernel,
        out_shape=(jax.ShapeDtypeStruct((b, nt, HEADS * QH), BF16),
                   jax.ShapeDtypeStruct((b, nt, QH), BF16),
                   jax.ShapeDtypeStruct((b, nt, 2 * LANES), BF16)),
        grid=(b, nt // TM),
        in_specs=[pl.BlockSpec((None, TM, D_ZA), lambda bi, i: (bi, i, 0)),
                  pl.BlockSpec((1, Q_LORA), lambda bi, i: (0, 0)),
                  pl.BlockSpec((1, KV_LORA), lambda bi, i: (0, 0)),
                  pl.BlockSpec((Q_LORA, 2 * HEADS * QH), lambda bi, i: (0, 0)),
                  pl.BlockSpec((TM, QH), lambda bi, i: (i, 0)),
                  pl.BlockSpec((TM, QH), lambda bi, i: (i, 0)),
                  pl.BlockSpec((TM, 2 * QK_ROPE), lambda bi, i: (i, 0))],
        out_specs=(pl.BlockSpec((None, TM, HEADS * QH), lambda bi, i: (bi, i, 0)),
                   pl.BlockSpec((None, TM, QH), lambda bi, i: (bi, i, 0)),
                   pl.BlockSpec((None, TM, 2 * LANES), lambda bi, i: (bi, i, 0))),
        compiler_params=_cparams(("parallel", "parallel"), VMEM_LIMIT),
        name="mla_proj",
    )(za, qn, kvn, wq, cq, sq, ck)


def _attn_kernel(q_ref, kv_ref, v1_ref, o_ref, q_s, m_s, acc_s, s_a, s_b, s_c,
                 *, n_ctx, n_lat, n_ctx_qsteps):
    i = pl.program_id(1)
    m_s[...] = jnp.full(m_s.shape, NEG_INF, F32)
    acc_s[...] = jnp.zeros(acc_s.shape, F32)
    for c in range(ATT_CHAINS):
        for h in range(HEADS):
            q_s[c, h * TQ:(h + 1) * TQ, :] = q_ref[c * TQ:(c + 1) * TQ, h * QH:(h + 1) * QH]

    def scores(c, s_ref, k0, size):
        k = kv_ref[pl.ds(k0, size), :]
        s_ref[c, :, :size] = lax.dot_general(q_s[c], k, (((1,), (1,)), ((), ())),
                                             preferred_element_type=F32)

    def softmax_pv(c, s_ref, k0, size):
        s = s_ref[c, :, :size]
        m_old = m_s[c]
        m_new = jnp.maximum(m_old, jnp.max(s, axis=1, keepdims=True))
        alpha = jnp.exp2(m_old - m_new)
        p = jnp.exp2(s - m_new).astype(BF16)
        acc_s[c] = alpha * acc_s[c] + jnp.dot(p, v1_ref[pl.ds(k0, size), :],
                                              preferred_element_type=F32)
        m_s[c] = m_new

    chains = range(ATT_CHAINS)

    @pl.when(i < n_ctx_qsteps)
    def _():
        for c in chains:
            scores(c, s_c, 0, n_ctx)
        for c in chains:
            softmax_pv(c, s_c, 0, n_ctx)

    tiles = [(s_c, 0, n_ctx)] + [((s_a, s_b)[t % 2], n_ctx + t * TK, TK) for t in range(n_lat // TK)]

    @pl.when(i >= n_ctx_qsteps)
    def _():
        for c in chains:
            scores(c, *tiles[0])
        for t, tile in enumerate(tiles):
            if t + 1 < len(tiles):
                for c in chains:
                    scores(c, *tiles[t + 1])
            for c in chains:
                softmax_pv(c, *tile)

    for c in chains:
        acc = acc_s[c]
        o = (acc[:, :KV_LORA] / acc[:, KV_LORA:KV_LORA + 1]).astype(BF16)
        for h in range(HEADS):
            o_ref[c * TQ:(c + 1) * TQ, h * KV_LORA:(h + 1) * KV_LORA] = o[h * TQ:(h + 1) * TQ, :]


def _attention(q, kv, v1, n_ctx):
    b, nt, _ = kv.shape
    rows = TQ * HEADS
    step = ATT_CHAINS * TQ
    n_lat = nt - n_ctx
    assert n_lat % TK == 0 and n_ctx % step == 0 and n_lat % step == 0
    kern = functools.partial(_attn_kernel, n_ctx=n_ctx, n_lat=n_lat, n_ctx_qsteps=n_ctx // step)
    return pl.pallas_call(
        kern,
        out_shape=jax.ShapeDtypeStruct((b, nt, HEADS * KV_LORA), BF16),
        grid=(b, nt // step),
        in_specs=[pl.BlockSpec((None, step, HEADS * QH), lambda bi, i: (bi, i, 0)),
                  pl.BlockSpec((None, nt, QH), lambda bi, i: (bi, 0, 0)),
                  pl.BlockSpec((None, nt, 2 * LANES), lambda bi, i: (bi, 0, 0))],
        out_specs=pl.BlockSpec((None, step, HEADS * KV_LORA), lambda bi, i: (bi, i, 0)),
        scratch_shapes=[pltpu.VMEM((ATT_CHAINS, rows, QH), BF16),
                        pltpu.VMEM((ATT_CHAINS, rows, 1), F32),
                        pltpu.VMEM((ATT_CHAINS, rows, 2 * LANES), F32),
                        pltpu.VMEM((ATT_CHAINS, rows, TK), F32),
                        pltpu.VMEM((ATT_CHAINS, rows, TK), F32),
                        pltpu.VMEM((ATT_CHAINS, rows, n_ctx), F32)],
        compiler_params=_cparams(("parallel", "arbitrary"), VMEM_LIMIT),
        name="mla_attention",
    )(q, kv, v1)


def _lru_tile(j, nct, ntl, reverse):
    if not reverse:
        return j
    return jnp.where(j < nct, nct - 1 - j, ntl - 1 - (j - nct))


def _lru_kernel(x_ref, prev_ref, next_ref, cw_ref, cb_ref, w_ref, ba_ref, bx_ref, lam_ref,
                o_ref, a_s, u_s, h_s, *, nct, ntl, reverse):
    j = pl.program_id(1)
    tile = _lru_tile(j, nct, ntl, reverse)

    @pl.when(j == 0)
    def _():
        h_s[...] = jnp.zeros(h_s.shape, F32)

    x = x_ref[...].astype(F32)
    has_prev = jnp.logical_and(tile != 0, tile != nct)
    has_next = jnp.logical_and(tile != nct - 1, tile != ntl - 1)
    prev = jnp.where(has_prev, prev_ref[...].astype(F32), 0.0)
    nxt = jnp.where(has_next, next_ref[...].astype(F32), 0.0)
    row = lax.broadcasted_iota(jnp.int32, (TM, D), 0)
    xm1 = jnp.where(row == 0, prev[HALO - 1:HALO, :], pltpu.roll(x, 1, 0))
    xm2 = jnp.where(row == 0, prev[HALO - 2:HALO - 1, :],
                    jnp.where(row == 1, prev[HALO - 1:HALO, :], pltpu.roll(x, 2, 0)))
    xp1 = jnp.where(row == TM - 1, nxt[0:1, :], pltpu.roll(x, TM - 1, 0))
    cw = cw_ref[...]
    xl = cb_ref[...] + xm2 * cw[0:1, :] + xm1 * cw[1:2, :] + x * cw[2:3, :] + xp1 * cw[3:4, :]

    z = -lam_ref[...]
    softplus = jnp.maximum(z, 0.0) + jnp.log1p(jnp.exp(-jnp.abs(z)))
    rate = -LRU_C * softplus
    for g in range(D // LANES):
        sl = slice(g * LANES, (g + 1) * LANES)
        xg = xl[:, sl]
        pre = jnp.dot(xg.astype(BF16), w_ref[g], preferred_element_type=F32)
        r = _sigmoid(pre[:, :LANES] + ba_ref[:, sl])
        ig = _sigmoid(pre[:, LANES:] + bx_ref[:, sl])
        log_a = rate[:, sl] * r
        a = jnp.exp(log_a)
        a_s[:, sl] = a
        u_s[:, sl] = jnp.sqrt(-jnp.tanh(log_a) * (1.0 + a * a)) * (ig * xg)

    srow = lax.broadcasted_iota(jnp.int32, (SUBLANES, D), 0)
    n_chunks = TM // SUBLANES

    def chunk(c, h_prev):
        cc = (n_chunks - 1 - c) if reverse else c
        r0 = pl.multiple_of(cc * SUBLANES, SUBLANES)
        a = a_s[pl.ds(r0, SUBLANES), :]
        u = u_s[pl.ds(r0, SUBLANES), :]
        for s in (1, 2, 4):
            if reverse:
                keep = srow < SUBLANES - s
                shift = SUBLANES - s
            else:
                keep = srow >= s
                shift = s
            u = u + a * jnp.where(keep, pltpu.roll(u, shift, 0), 0.0)
            a = a * jnp.where(keep, pltpu.roll(a, shift, 0), 1.0)
        h = u + a * h_prev
        o_ref[pl.ds(r0, SUBLANES), :] = h
        edge = h[0:1, :] if reverse else h[SUBLANES - 1:SUBLANES, :]
        return jnp.broadcast_to(edge, (SUBLANES, D))

    h_s[...] = lax.fori_loop(0, n_chunks, chunk, h_s[...])


def _lru(zb, cw, cb, wd, ba, bx, lam, nct, reverse):
    b, nt, _ = zb.shape
    ntl = nt // TM
    hb = TM // HALO
    n_hb = nt // HALO
    tile = lambda j: _lru_tile(j, nct, ntl, reverse)
    kern = functools.partial(_lru_kernel, nct=nct, ntl=ntl, reverse=reverse)
    vec = lambda n: pl.BlockSpec((n, D), lambda bi, j: (0, 0))
    return pl.pallas_call(
        kern,
        out_shape=jax.ShapeDtypeStruct((b, nt, D), F32),
        grid=(b, ntl),
        in_specs=[pl.BlockSpec((None, TM, D), lambda bi, j: (bi, tile(j), 0)),
                  pl.BlockSpec((None, HALO, D),
                               lambda bi, j: (bi, jnp.maximum(tile(j) * hb - 1, 0), 0)),
                  pl.BlockSpec((None, HALO, D),
                               lambda bi, j: (bi, jnp.minimum((tile(j) + 1) * hb, n_hb - 1), 0)),
                  vec(CONV_W), vec(1),
                  pl.BlockSpec((D // LANES, LANES, 2 * LANES), lambda bi, j: (0, 0, 0)),
                  vec(1), vec(1), vec(1)],
        out_specs=pl.BlockSpec((None, TM, D), lambda bi, j: (bi, tile(j), 0)),
        scratch_shapes=[pltpu.VMEM((TM, D), F32), pltpu.VMEM((TM, D), F32),
                        pltpu.VMEM((SUBLANES, D), F32)],
        compiler_params=_cparams(("arbitrary", "arbitrary"), VMEM_LIMIT),
        name="rglru_bwd" if reverse else "rglru_fwd",
    )(zb, zb, zb, cw, cb, wd, ba, bx, lam)


def _merge_kernel(o_ref, hf_ref, hb_ref, gr_ref, ga_ref, gl_ref, x_ref, mod_ref, bg_ref, nf_ref,
                  wvo_ref, wol_ref, wout_ref, wr_ref, br_ref,
                  xn_ref, h2_ref, rg_ref, rt_ref, meta_ref, cnt_ref, carry_s):
    first = jnp.logical_and(pl.program_id(0) == 0, pl.program_id(1) == 0)

    @pl.when(first)
    def _():
        carry_s[...] = jnp.zeros(carry_s.shape, F32)

    wr = wr_ref[...]
    w_hi = wr.astype(BF16)
    w_lo = (wr - w_hi.astype(F32)).astype(BF16)

    gr = gr_ref[...].astype(F32)
    gelu = 0.5 * gr * (1.0 + jnp.tanh(math.sqrt(2.0 / math.pi) * (gr + 0.044715 * (gr * gr * gr))))
    lru = ((hf_ref[...] + hb_ref[...]) * gelu).astype(BF16)
    y_att = jnp.dot(o_ref[...], wvo_ref[...], preferred_element_type=F32)
    y_lru = jnp.dot(lru, wol_ref[...], preferred_element_type=F32)
    m = (_sigmoid(ga_ref[...].astype(F32) + bg_ref[0]) * y_att
         + _sigmoid(gl_ref[...].astype(F32) + bg_ref[1]) * y_lru)
    out = jnp.dot(m.astype(BF16), wout_ref[...], preferred_element_type=F32)
    xn = x_ref[...] + mod_ref[2] * out
    xn_ref[...] = xn
    h2 = _rms(xn, nf_ref[...]) * (1.0 + mod_ref[4]) + mod_ref[3]
    h_hi = h2.astype(BF16)
    h2_ref[...] = h_hi
    h_lo = (h2 - h_hi.astype(F32)).astype(BF16)
    logits = (jnp.dot(h_hi, w_hi, preferred_element_type=F32)
              + (jnp.dot(h_lo, w_hi, preferred_element_type=F32)
                 + jnp.dot(h_hi, w_lo, preferred_element_type=F32))) + br_ref[...]
    lane = lax.broadcasted_iota(jnp.int32, (TM, LANES), 1)
    lane_f = lane.astype(F32)
    l = logits
    vals, idxs = [], []
    for _ in range(TOP_K):
        mx = jnp.max(l, axis=1, keepdims=True)
        ix = jnp.min(jnp.where(l == mx, lane_f, float(LANES)), axis=1, keepdims=True)
        vals.append(mx)
        idxs.append(ix)
        l = jnp.where(lane_f == ix, NEG_INF, l)
    chosen = jnp.where(jnp.logical_and(l == NEG_INF, lane < N_EXPERTS), 1.0, 0.0)
    es = [jnp.exp(v - vals[0]) for v in vals]
    den = es[0] + es[1] + es[2] + es[3]

    r_i = lax.broadcasted_iota(jnp.int32, (TM, TM), 0)
    c_i = lax.broadcasted_iota(jnp.int32, (TM, TM), 1)
    tri = jnp.where(r_i > c_i, 1.0, 0.0).astype(BF16)
    before = jnp.dot(tri, chosen.astype(BF16), preferred_element_type=F32)
    cnt_t = jnp.sum(chosen, axis=0, keepdims=True)
    seg_len = jnp.floor((cnt_t + (MOE_CH - 1.0)) * (1.0 / MOE_CH)) * MOE_CH
    e_r = lax.broadcasted_iota(jnp.int32, (LANES, LANES), 0)
    e_c = lax.broadcasted_iota(jnp.int32, (LANES, LANES), 1)
    upper = jnp.where(e_r < e_c, 1.0, 0.0).astype(BF16)
    seg_off = jnp.dot(jnp.broadcast_to(seg_len, (SUBLANES, LANES)).astype(BF16), upper,
                      preferred_element_type=F32)[0:1, :]
    slot_of = before + seg_off
    rg = jnp.zeros((TM, LANES), F32)
    for k in range(TOP_K):
        slot = jnp.sum(jnp.where(lane_f == idxs[k], slot_of, 0.0), axis=1, keepdims=True)
        rg = jnp.where(lane == k, es[k] / den, rg)
        rg = jnp.where(lane == TOP_K + k, slot, rg)
    rg_ref[...] = rg
    rt_ref[...] = jnp.transpose(rg)[0:SUBLANES, :]
    row8 = lax.broadcasted_iota(jnp.int32, (SUBLANES, LANES), 0)
    carry = carry_s[...]
    meta_ref[...] = jnp.where(row8 == 0, cnt_t, jnp.where(row8 == 1, carry, 0.0)).astype(jnp.int32)
    carry = carry + jnp.floor((cnt_t + (SUBLANES - 1.0)) * (1.0 / SUBLANES)) * SUBLANES
    carry_s[...] = carry
    cnt_ref[...] = jnp.broadcast_to(carry, cnt_ref.shape).astype(jnp.int32)


def _merge(o, hf, hb, zb, x, mods, bg, nf, wvo, wol, wout, wr, br, nct, tile0):
    b, nt, _ = x.shape
    ntl = nt // TM
    tok = lambda c: pl.BlockSpec((None, TM, D), lambda bi, j: (bi, j + tile0, c))
    full = lambda *s: pl.BlockSpec(s, lambda bi, j: (0,) * len(s))
    out_nt = nt - tile0 * TM
    return pl.pallas_call(
        _merge_kernel,
        out_shape=(jax.ShapeDtypeStruct((b, out_nt, D), F32),
                   jax.ShapeDtypeStruct((b, out_nt, D), BF16),
                   jax.ShapeDtypeStruct((b, out_nt, LANES), F32),
                   jax.ShapeDtypeStruct((b, out_nt // TM, SUBLANES, TM), F32),
                   jax.ShapeDtypeStruct((b, out_nt // TM, SUBLANES, LANES), jnp.int32),
                   jax.ShapeDtypeStruct((SUBLANES, LANES), jnp.int32)),
        grid=(b, ntl - tile0),
        in_specs=[tok(0), tok(0), tok(0), tok(1), tok(2), tok(3), tok(0),
                  pl.BlockSpec((None, None, 6, 1, D),
                               lambda bi, j: (bi, _seg(j + tile0, nct), 0, 0, 0)),
                  full(2, 1, D), full(1, D), full(D, D), full(D, D), full(D, D),
                  full(D, LANES), full(1, LANES)],
        out_specs=(pl.BlockSpec((None, TM, D), lambda bi, j: (bi, j, 0)),
                   pl.BlockSpec((None, TM, D), lambda bi, j: (bi, j, 0)),
                   pl.BlockSpec((None, TM, LANES), lambda bi, j: (bi, j, 0)),
                   pl.BlockSpec((None, None, SUBLANES, TM), lambda bi, j: (bi, j, 0, 0)),
                   pl.BlockSpec((None, None, SUBLANES, LANES), lambda bi, j: (bi, j, 0, 0)),
                   pl.BlockSpec((SUBLANES, LANES), lambda bi, j: (0, 0))),
        scratch_shapes=[pltpu.VMEM((1, LANES), F32)],
        compiler_params=_cparams(("arbitrary", "arbitrary"), VMEM_LIMIT),
        name="merge_route",
    )(o, hf, hb, zb, zb, zb, x, mods, bg, nf, wvo, wol, wout, wr, br)


def _dispatch_kernel(meta_ref, h2_ref, rt_ref, xs_ref, sorted_s, zero_s, sem, *, n_blk, ntl):
    bi = pl.program_id(0)
    j = pl.program_id(1)
    first = jnp.logical_and(bi == 0, j == 0)

    @pl.when(first)
    def _():
        zero_s[...] = jnp.zeros(zero_s.shape, F32)

        def pad_copy(i):
            r0 = pl.multiple_of(i * SUBLANES, SUBLANES)
            return pltpu.make_async_copy(zero_s.at[pl.ds(0, SUBLANES), :],
                                         xs_ref.at[pl.ds(r0, SUBLANES), :], sem)

        def blk_copy(i):
            r0 = pl.multiple_of(i * MOE_R, MOE_R)
            return pltpu.make_async_copy(zero_s, xs_ref.at[pl.ds(r0, MOE_R), :], sem)

        def blk_issue(i, c):
            blk_copy(i).start()
            return c

        def blk_drain(i, c):
            blk_copy(i).wait()
            return c
        n_used = meta_ref[META_NUSED]
        lax.fori_loop(n_used, n_blk, blk_issue, 0)
        lax.fori_loop(n_used, n_blk, blk_drain, 0)

        def per_expert(e, c):
            lo = meta_ref[META_TAIL + e] // SUBLANES
            hi = meta_ref[META_END + e] // SUBLANES

            def issue(r, c2):
                pad_copy(r).start()
                return c2

            def drain(r, c2):
                pad_copy(r).wait()
                return c2
            lax.fori_loop(lo, hi, issue, 0)
            lax.fori_loop(lo, hi, drain, 0)
            return c
        lax.fori_loop(0, N_EXPERTS, per_expert, 0)

    slot_i = lax.broadcasted_iota(jnp.int32, (MOE_SLOTS, TM), 0).astype(F32)
    hit = slot_i == rt_ref[TOP_K:TOP_K + 1, :]
    for k in range(1, TOP_K):
        hit = jnp.logical_or(hit, slot_i == rt_ref[TOP_K + k:TOP_K + k + 1, :])
    onehot = jnp.where(hit, 1.0, 0.0).astype(BF16)
    tile = bi * ntl + j
    cur = tile % 2
    sorted_s[cur] = jnp.dot(onehot, h2_ref[...], preferred_element_type=F32)

    def refs_of(buf):
        return lambda slot, row: (sorted_s.at[buf, pl.ds(slot, MOE_CH), :],
                                  xs_ref.at[pl.ds(row, MOE_CH), :])

    @pl.when(tile > 0)
    def _():
        _moe_chunk_copies(meta_ref, tile - 1, sem, refs_of(1 - cur), start=False)
    _moe_chunk_copies(meta_ref, tile, sem, refs_of(cur), start=True)

    @pl.when(tile == pl.num_programs(0) * ntl - 1)
    def _():
        _moe_chunk_copies(meta_ref, tile, sem, refs_of(cur), start=False)


def _moe_chunk_copies(meta_ref, tile, sem, refs, *, start):
    base = META_TILES + tile * (2 * N_EXPERTS)

    def per_expert(e, slot0):
        n = meta_ref[base + e]
        row0 = meta_ref[e] + meta_ref[base + N_EXPERTS + e]
        n_ch = (n + MOE_CH - 1) // MOE_CH

        def piece(c, carry):
            src, dst = refs(pl.multiple_of(slot0 + c * MOE_CH, MOE_CH),
                            pl.multiple_of(row0 + c * MOE_CH, SUBLANES))
            cp = pltpu.make_async_copy(src, dst, sem)
            if start:
                cp.start()
            else:
                cp.wait()
            return carry
        lax.fori_loop(0, n_ch, piece, 0)
        return slot0 + n_ch * MOE_CH
    lax.fori_loop(0, N_EXPERTS, per_expert, 0)


def _dispatch(meta, h2, rt, n_rows):
    b, nt, _ = h2.shape
    ntl = nt // TM
    kern = functools.partial(_dispatch_kernel, n_blk=n_rows // MOE_R, ntl=ntl)
    return pl.pallas_call(
        kern,
        out_shape=jax.ShapeDtypeStruct((n_rows, D), F32),
        grid_spec=pltpu.PrefetchScalarGridSpec(
            num_scalar_prefetch=1,
            grid=(b, ntl),
            in_specs=[pl.BlockSpec((None, TM, D), lambda bi, j, meta: (bi, j, 0)),
                      pl.BlockSpec((None, None, SUBLANES, TM), lambda bi, j, meta: (bi, j, 0, 0))],
            out_specs=pl.BlockSpec(memory_space=pl.ANY),
            scratch_shapes=[pltpu.VMEM((2, MOE_SLOTS, D), F32),
                            pltpu.VMEM((MOE_R, D), F32),
                            pltpu.SemaphoreType.DMA]),
        compiler_params=_cparams(("arbitrary", "arbitrary"), VMEM_LIMIT),
        name="moe_dispatch",
    )(meta, h2, rt)


def _expert_kernel(be_ref, nu_ref, x_ref, wg_ref, bg_ref, wu_ref, bu_ref, wd_ref, bd_ref,
                   y_ref, wg_s, wu_s, wd_s):
    blk = pl.program_id(0)
    used = blk < nu_ref[0]
    prev = be_ref[jnp.maximum(blk - 1, 0)]
    fresh = jnp.logical_or(blk == 0, be_ref[blk] != prev)

    @pl.when(jnp.logical_and(used, fresh))
    def _():
        wg_s[...] = wg_ref[...].astype(BF16)
        wu_s[...] = wu_ref[...].astype(BF16)
        wd_s[...] = wd_ref[...].astype(BF16)

    @pl.when(used)
    def _():
        x = x_ref[...].astype(BF16)
        g = jnp.dot(x, wg_s[...], preferred_element_type=F32) + bg_ref[...]
        u = jnp.dot(x, wu_s[...], preferred_element_type=F32) + bu_ref[...]
        g = jnp.minimum(g, SWIGLU_LIMIT)
        u = jnp.clip(u, -SWIGLU_LIMIT, SWIGLU_LIMIT)
        act = g * _sigmoid(SWIGLU_ALPHA * g) * (u + 1.0)
        y_ref[...] = jnp.dot(act.astype(BF16), wd_s[...], preferred_element_type=F32) + bd_ref[...]

    @pl.when(jnp.logical_not(used))
    def _():
        y_ref[...] = jnp.zeros(y_ref.shape, F32)


def _experts(blk_expert, n_used, xs, wg, bg, wu, bu, wd, bd, layer):
    n_rows = xs.shape[0]
    n_blk = n_rows // MOE_R
    dff = wg.shape[-1]
    e_of = lambda i, be, nu: be[jnp.minimum(i, nu[0] - 1)]
    wspec = lambda r, c: pl.BlockSpec((None, None, r, c), lambda i, be, nu: (layer, e_of(i, be, nu), 0, 0))
    return pl.pallas_call(
        _expert_kernel,
        out_shape=jax.ShapeDtypeStruct((n_rows, D), F32),
        grid_spec=pltpu.PrefetchScalarGridSpec(
            num_scalar_prefetch=2,
            grid=(n_blk,),
            in_specs=[pl.BlockSpec((MOE_R, D), lambda i, be, nu: (jnp.minimum(i, nu[0] - 1), 0)),
                      wspec(D, dff), wspec(1, dff), wspec(D, dff), wspec(1, dff),
                      wspec(dff, D), wspec(1, D)],
            out_specs=pl.BlockSpec((MOE_R, D), lambda i, be, nu: (i, 0)),
            scratch_shapes=[pltpu.VMEM((D, dff), BF16), pltpu.VMEM((D, dff), BF16),
                            pltpu.VMEM((dff, D), BF16)]),
        compiler_params=_cparams(("arbitrary",), VMEM_LIMIT),
        name="moe_experts",
    )(blk_expert, n_used, xs, wg, bg, wu, bu, wd, bd)


def _combine_kernel(meta_ref, x_ref, rg_ref, mod_ref, fn_ref, ys_ref, o_ref, buf_s, sem, *, final, ntl):
    bi = pl.program_id(0)
    j = pl.program_id(1)

    tile = bi * ntl + j
    cur = tile % 2

    def refs_of(buf):
        return lambda slot, row: (ys_ref.at[pl.ds(row, MOE_CH), :],
                                  buf_s.at[buf, pl.ds(slot, MOE_CH), :])

    @pl.when(tile == 0)
    def _():
        buf_s[...] = jnp.zeros(buf_s.shape, F32)
        _moe_chunk_copies(meta_ref, tile, sem, refs_of(cur), start=True)

    _moe_chunk_copies(meta_ref, tile, sem, refs_of(cur), start=False)

    @pl.when(tile < pl.num_programs(0) * ntl - 1)
    def _():
        _moe_chunk_copies(meta_ref, tile + 1, sem, refs_of(1 - cur), start=True)

    rg = rg_ref[...]
    slot_i = lax.broadcasted_iota(jnp.int32, (TM, MOE_SLOTS), 1).astype(F32)
    gate = jnp.zeros((TM, MOE_SLOTS), F32)
    for k in range(TOP_K):
        gate = jnp.where(slot_i == rg[:, TOP_K + k:TOP_K + k + 1], rg[:, k:k + 1], gate)
    y = jnp.dot(gate.astype(BF16), buf_s[cur].astype(BF16), preferred_element_type=F32)
    xo = x_ref[...] + mod_ref[5] * y
    if final:
        xo = _rms(xo, fn_ref[...])
    o_ref[...] = xo


def _combine(meta, x, rg, mods, fn, ys, nct, final):
    b, nt, _ = x.shape
    ntl = nt // TM
    kern = functools.partial(_combine_kernel, final=final, ntl=ntl)
    return pl.pallas_call(
        kern,
        out_shape=jax.ShapeDtypeStruct((b, nt, D), F32),
        grid_spec=pltpu.PrefetchScalarGridSpec(
            num_scalar_prefetch=1,
            grid=(b, ntl),
            in_specs=[pl.BlockSpec((None, TM, D), lambda bi, j, meta: (bi, j, 0)),
                      pl.BlockSpec((None, TM, LANES), lambda bi, j, meta: (bi, j, 0)),
                      pl.BlockSpec((None, None, 6, 1, D),
                                   lambda bi, j, meta: (bi, _seg(j, nct), 0, 0, 0)),
                      pl.BlockSpec((1, D), lambda bi, j, meta: (0, 0)),
                      pl.BlockSpec(memory_space=pl.ANY)],
            out_specs=pl.BlockSpec((None, TM, D), lambda bi, j, meta: (bi, j, 0)),
            scratch_shapes=[pltpu.VMEM((2, MOE_SLOTS, D), F32), pltpu.SemaphoreType.DMA]),
        compiler_params=_cparams(("arbitrary", "arbitrary"), VMEM_LIMIT),
        name="moe_combine",
    )(meta, x, rg, mods, fn, ys)


_ROT_PARTNER = tuple(list(range(16, 32)) + list(range(0, 16)) + list(range(48, 64)) + list(range(32, 48)))


def _rope_tables(n_ctx, n_lat):
    n = jnp.arange(n_lat)
    row = (n // GRID_W).astype(F32)
    col = (n % GRID_W).astype(F32)
    n_freq = QK_ROPE // 4
    inv = ROPE_THETA ** (-jnp.arange(n_freq, dtype=F32) / n_freq)
    ar = row[:, None] * inv
    ac = col[:, None] * inv
    c64 = jnp.concatenate([jnp.cos(ar), jnp.cos(ar), jnp.cos(ac), jnp.cos(ac)], axis=1)
    s64 = jnp.concatenate([-jnp.sin(ar), jnp.sin(ar), -jnp.sin(ac), jnp.sin(ac)], axis=1)
    c64 = jnp.concatenate([jnp.ones((n_ctx, QK_ROPE), F32), c64], axis=0)
    s64 = jnp.concatenate([jnp.zeros((n_ctx, QK_ROPE), F32), s64], axis=0)
    nt = n_ctx + n_lat
    qs = ATTN_SCALE * LOG2E
    cq = qs * jnp.concatenate([jnp.ones((nt, QK_NOPE), F32), c64, c64], axis=1)
    sq = qs * jnp.concatenate([jnp.zeros((nt, QK_NOPE), F32), s64, s64], axis=1)
    ck = jnp.concatenate([c64, s64], axis=1)
    return cq, sq, ck


def _block_diag_pairs(w):
    w = w.reshape(LRU_BLOCKS // 2, 2, LRU_BW, LRU_BW)
    z = jnp.zeros((LRU_BLOCKS // 2, LRU_BW, LRU_BW), w.dtype)
    top = jnp.concatenate([w[:, 0], z], axis=2)
    bot = jnp.concatenate([z, w[:, 1]], axis=2)
    return jnp.concatenate([top, bot], axis=1)


def kernel(x, c, ctx, c_ctx, w_ada, b_ada, norm_mix, norm_ffn, w_in, b_branch_gate, q_norm, w_uq, kv_norm, w_ukv, w_o_attn, conv_w, conv_b, lru_w_a, lru_b_a, lru_w_x, lru_b_x, lru_lambda, w_o_lru, w_out, w_router, b_router, w_exp_gate, b_exp_gate, w_exp_up, b_exp_up, w_exp_down, b_exp_down, final_norm):
    depth = w_in.shape[0]
    b, n_lat, _ = x.shape
    n_ctx = ctx.shape[1]
    assert n_ctx % TM == 0 and n_lat % TK == 0 and b + 1 <= SUBLANES
    nct = n_ctx // TM
    nt = n_ctx + n_lat
    ntl = nt // TM
    perm = jnp.array(_ROT_PARTNER)

    xs = jnp.concatenate([ctx, x], axis=1)
    cond8 = jnp.zeros((SUBLANES, D), F32).at[:b].set(c).at[b].set(c_ctx)
    cq_t, sq_t, ck_t = _rope_tables(n_ctx, n_lat)

    for l in range(depth):
        last = l == depth - 1
        tile0 = nct if last else 0

        m8 = _ada(cond8, w_ada[l], b_ada[l][None, :]).reshape(SUBLANES, 6, 1, D)
        mods = jnp.stack([jnp.broadcast_to(m8[b], (b,) + m8.shape[1:]), m8[:b]], axis=1)

        wi = w_in[l]
        kpe_w = wi[:, Q_LORA + KV_LORA:Q_LORA + KV_LORA + QK_ROPE]
        wa = jnp.concatenate([wi[:, :Q_LORA + KV_LORA + QK_ROPE], kpe_w[:, perm]], axis=1).astype(BF16)
        wb = wi[:, Q_LORA + KV_LORA + QK_ROPE:].astype(BF16)
        uq = w_uq[l].reshape(Q_LORA, HEADS, QK_DIM).transpose(1, 0, 2)
        ukv = w_ukv[l].reshape(KV_LORA, HEADS, QK_NOPE + V_HEAD).transpose(1, 0, 2)
        wqa, wvo = _fold(uq[:, :, :QK_NOPE], ukv[:, :, :QK_NOPE], ukv[:, :, QK_NOPE:],
                         w_o_attn[l].reshape(HEADS, V_HEAD, D))
        pe = uq[:, :, QK_NOPE:]
        pe_sw = pe[:, :, perm]
        zq = jnp.zeros((HEADS, Q_LORA, QK_NOPE), F32)
        wq1 = jnp.concatenate([wqa, pe, pe], axis=2)
        wq2 = jnp.concatenate([zq, pe_sw, pe_sw], axis=2)
        wq = jnp.concatenate([wq1.transpose(1, 0, 2).reshape(Q_LORA, HEADS * QH),
                              wq2.transpose(1, 0, 2).reshape(Q_LORA, HEADS * QH)], axis=1).astype(BF16)
        wvo = wvo.reshape(D, D).astype(BF16)
        wr = jnp.zeros((D, LANES), F32).at[:, :N_EXPERTS].set(w_router[l])
        br = jnp.full((1, LANES), NEG_INF, F32).at[0, :N_EXPERTS].set(b_router[l])

        za, zb = _inproj(xs, mods, norm_mix[l][None, :], wa, wb, nct)
        q, kv, v1 = _mlaproj(za, q_norm[l][None, :], kv_norm[l][None, :], wq, cq_t, sq_t, ck_t)
        o = _attention(q, kv, v1, n_ctx)
        hs = []
        for d in range(2):
            wd = jnp.concatenate([_block_diag_pairs(lru_w_a[l, d]), _block_diag_pairs(lru_w_x[l, d])],
                                 axis=2).astype(BF16)
            hs.append(_lru(zb, conv_w[l], conv_b[l][None, :], wd, lru_b_a[l, d][None, :],
                           lru_b_x[l, d][None, :], lru_lambda[l, d][None, :], nct, reverse=d == 1))
        xn, h2, rg, rt, tile_meta, cnt = _merge(
            o, hs[0], hs[1], zb, xs, mods, b_branch_gate[l].reshape(2, 1, D), norm_ffn[l][None, :],
            wvo, w_o_lru[l].astype(BF16), w_out[l].astype(BF16), wr, br, nct, tile0)

        counts = cnt[0, :N_EXPERTS]
        padded = ((counts + (MOE_CH - 1) + MOE_R - 1) // MOE_R) * MOE_R
        pad_end = jnp.cumsum(padded)
        pad_start = pad_end - padded
        moe_nt = xn.shape[1]
        n_tiles = b * moe_nt // TM
        max_rows = (b * moe_nt * TOP_K + n_tiles * N_EXPERTS * (SUBLANES - 1)
                    + N_EXPERTS * (MOE_CH - 1))
        n_blk = max_rows // MOE_R + N_EXPERTS
        n_rows = n_blk * MOE_R
        blk_start = jnp.arange(n_blk, dtype=jnp.int32) * MOE_R
        blk_expert = jnp.minimum(jnp.sum(blk_start[:, None] >= pad_end[None, :], axis=1),
                                 N_EXPERTS - 1).astype(jnp.int32)
        n_used = (pad_end[-1:] // MOE_R).astype(jnp.int32)
        meta = jnp.concatenate([pad_start, pad_start + counts, pad_end,
                                jnp.broadcast_to(n_used, (N_EXPERTS,)),
                                tile_meta[:, :, 0:2, :N_EXPERTS].reshape(-1)]).astype(jnp.int32)

        xsort = _dispatch(meta, h2, rt, n_rows)
        ys = _experts(blk_expert, n_used, xsort, w_exp_gate, b_exp_gate[:, :, None, :],
                      w_exp_up, b_exp_up[:, :, None, :], w_exp_down, b_exp_down[:, :, None, :], l)
        xs = _combine(meta, xn, rg, mods, final_norm[None, :], ys, nct - tile0, last)
    return xs
```

```python
import functools
import math

import jax
import jax.numpy as jnp
from jax import lax
from jax.experimental import pallas as pl
from jax.experimental.pallas import tpu as pltpu

D = 1024
GRID_W = 64
HEADS = 8
QK_NOPE = 128
QK_ROPE = 64
QK_DIM = QK_NOPE + QK_ROPE
V_HEAD = 128
Q_LORA = 256
KV_LORA = 128
ROPE_THETA = 10000.0
ATTN_SCALE = QK_DIM ** -0.5
LRU_BLOCKS = 16
LRU_BW = D // LRU_BLOCKS
CONV_W = 4
LRU_C = 8.0
N_EXPERTS = 32
TOP_K = 4
SWIGLU_ALPHA = 1.702
SWIGLU_LIMIT = 7.0
NORM_EPS = 1e-6
D_ZA = Q_LORA + KV_LORA + 2 * QK_ROPE
D_ZB = 4 * D
QH = 256

LANES = 128
SUBLANES = 8
TM = 256
HALO = 16
TQ = 128
ATT_CHAINS = 1
TK = 1024
MOE_R = 256
MOE_CH = 16
MOE_SLOTS = TM * TOP_K + N_EXPERTS * MOE_CH
META_TAIL = N_EXPERTS
META_END = 2 * N_EXPERTS
META_NUSED = 3 * N_EXPERTS
META_TILES = 4 * N_EXPERTS
MOE_MAX_COPIES = MOE_SLOTS // MOE_CH
META_PER_TILE = 256
assert 1 + 2 * MOE_MAX_COPIES <= META_PER_TILE
VMEM_LIMIT = 56 * 1024 * 1024

F32 = jnp.float32
BF16 = jnp.bfloat16
NEG_INF = float("-inf")
LOG2E = 1.4426950408889634


def _cparams(sem, vmem=None):
    return pltpu.CompilerParams(dimension_semantics=sem, vmem_limit_bytes=vmem)


def _rms(x, g):
    return x * lax.rsqrt(jnp.mean(x * x, axis=-1, keepdims=True) + NORM_EPS) * g


def _sigmoid(x):
    return 0.5 * jnp.tanh(0.5 * x) + 0.5


def _seg(i, nct):
    return jnp.where(i < nct, 0, 1)


def _ada_kernel(c_ref, w_ref, b_ref, o_ref):
    c = c_ref[...]
    s = (c * _sigmoid(c)).astype(BF16)
    o_ref[...] = jnp.dot(s, w_ref[...].astype(BF16), preferred_element_type=F32) + b_ref[...]


def _ada(cond8, w, b, tn=768):
    n = w.shape[1]
    return pl.pallas_call(
        _ada_kernel,
        out_shape=jax.ShapeDtypeStruct((SUBLANES, n), F32),
        grid=(n // tn,),
        in_specs=[pl.BlockSpec((SUBLANES, D), lambda j: (0, 0)),
                  pl.BlockSpec((D, tn), lambda j: (0, j)),
                  pl.BlockSpec((1, tn), lambda j: (0, j))],
        out_specs=pl.BlockSpec((SUBLANES, tn), lambda j: (0, j)),
        compiler_params=_cparams(("parallel",)),
        name="adaln",
    )(cond8, w, b)


def _fold_kernel(nope_ref, wuk_ref, wuv_ref, wo_ref, wqa_ref, wvo_ref):
    hp = lax.Precision.HIGHEST
    wqa_ref[...] = lax.dot_general(nope_ref[...], wuk_ref[...], (((1,), (1,)), ((), ())),
                                   precision=hp, preferred_element_type=F32)
    wvo_ref[...] = jnp.dot(wuv_ref[...], wo_ref[...], precision=hp, preferred_element_type=F32)


def _fold(nope, wuk, wuv, wo):
    sq = lambda *s: pl.BlockSpec((None,) + s, lambda h: (h, 0, 0))
    return pl.pallas_call(
        _fold_kernel,
        out_shape=(jax.ShapeDtypeStruct((HEADS, Q_LORA, KV_LORA), F32),
                   jax.ShapeDtypeStruct((HEADS, KV_LORA, D), F32)),
        grid=(HEADS,),
        in_specs=[sq(Q_LORA, QK_NOPE), sq(KV_LORA, QK_NOPE), sq(KV_LORA, V_HEAD), sq(V_HEAD, D)],
        out_specs=(sq(Q_LORA, KV_LORA), sq(KV_LORA, D)),
        compiler_params=_cparams(("parallel",)),
        name="mla_fold",
    )(nope, wuk, wuv, wo)


def _inproj_kernel(x_ref, mod_ref, g_ref, wa_ref, wb_ref, za_ref, zb_ref):
    h = _rms(x_ref[...], g_ref[...]) * (1.0 + mod_ref[1]) + mod_ref[0]
    hb = h.astype(BF16)
    za_ref[...] = jnp.dot(hb, wa_ref[...], preferred_element_type=F32)
    for j in range(D_ZB // D):
        zb_ref[:, j * D:(j + 1) * D] = jnp.dot(
            hb, wb_ref[:, j * D:(j + 1) * D], preferred_element_type=F32).astype(BF16)


def _inproj(x, mods, g, wa, wb, nct):
    b, nt, _ = x.shape
    return pl.pallas_call(
        _inproj_kernel,
        out_shape=(jax.ShapeDtypeStruct((b, nt, D_ZA), F32),
                   jax.ShapeDtypeStruct((b, nt, D_ZB), BF16)),
        grid=(b, nt // TM),
        in_specs=[pl.BlockSpec((None, TM, D), lambda bi, i: (bi, i, 0)),
                  pl.BlockSpec((None, None, 6, 1, D), lambda bi, i: (bi, _seg(i, nct), 0, 0, 0)),
                  pl.BlockSpec((1, D), lambda bi, i: (0, 0)),
                  pl.BlockSpec((D, D_ZA), lambda bi, i: (0, 0)),
                  pl.BlockSpec((D, D_ZB), lambda bi, i: (0, 0))],
        out_specs=(pl.BlockSpec((None, TM, D_ZA), lambda bi, i: (bi, i, 0)),
                   pl.BlockSpec((None, TM, D_ZB), lambda bi, i: (bi, i, 0))),
        compiler_params=_cparams(("parallel", "parallel"), VMEM_LIMIT),
        name="in_proj",
    )(x, mods, g, wa, wb)


def _mlaproj_kernel(za_ref, qn_ref, kvn_ref, wq_ref, cq_ref, sq_ref, ck_ref, q_ref, kv_ref, v1_ref):
    za = za_ref[...]
    cq = _rms(za[:, :Q_LORA], qn_ref[...]).astype(BF16)
    r = jnp.dot(cq, wq_ref[...], preferred_element_type=F32)
    cqt = cq_ref[...]
    sqt = sq_ref[...]
    for h in range(HEADS):
        a = r[:, h * QH:(h + 1) * QH]
        s = r[:, (HEADS + h) * QH:(HEADS + h + 1) * QH]
        q_ref[:, h * QH:(h + 1) * QH] = (a * cqt + s * sqt).astype(BF16)
    ckv = _rms(za[:, Q_LORA:Q_LORA + KV_LORA], kvn_ref[...])
    kv_ref[:, :KV_LORA] = ckv.astype(BF16)
    kv_ref[:, KV_LORA:] = (za[:, Q_LORA + KV_LORA:] * ck_ref[...]).astype(BF16)
    lane = lax.broadcasted_iota(jnp.int32, (TM, LANES), 1)
    v1_ref[:, :KV_LORA] = ckv.astype(BF16)
    v1_ref[:, KV_LORA:] = jnp.where(lane == 0, 1.0, 0.0).astype(BF16)


def _mlaproj(za, qn, kvn, wq, cq, sq, ck):
    b, nt, _ = za.shape
    return pl.pallas_call(
        _mlaproj_kernel,
        out_shape=(jax.ShapeDtypeStruct((b, nt, HEADS * QH), BF16),
                   jax.ShapeDtypeStruct((b, nt, QH), BF16),
                   jax.ShapeDtypeStruct((b, nt, 2 * LANES), BF16)),
        grid=(b, nt // TM),
        in_specs=[pl.BlockSpec((None, TM, D_ZA), lambda bi, i: (bi, i, 0)),
                  pl.BlockSpec((1, Q_LORA), lambda bi, i: (0, 0)),
                  pl.BlockSpec((1, KV_LORA), lambda bi, i: (0, 0)),
                  pl.BlockSpec((Q_LORA, 2 * HEADS * QH), lambda bi, i: (0, 0)),
                  pl.BlockSpec((TM, QH), lambda bi, i: (i, 0)),
                  pl.BlockSpec((TM, QH), lambda bi, i: (i, 0)),
                  pl.BlockSpec((TM, 2 * QK_ROPE), lambda bi, i: (i, 0))],
        out_specs=(pl.BlockSpec((None, TM, HEADS * QH), lambda bi, i: (bi, i, 0)),
                   pl.BlockSpec((None, TM, QH), lambda bi, i: (bi, i, 0)),
                   pl.BlockSpec((None, TM, 2 * LANES), lambda bi, i: (bi, i, 0))),
        compiler_params=_cparams(("parallel", "parallel"), VMEM_LIMIT),
        name="mla_proj",
    )(za, qn, kvn, wq, cq, sq, ck)


def _attn_kernel(q_ref, kv_ref, v1_ref, o_ref, q_s, m_s, acc_s, s_a, s_b, s_c,
                 *, n_ctx, n_lat, n_ctx_qsteps):
    i = pl.program_id(1)
    m_s[...] = jnp.full(m_s.shape, NEG_INF, F32)
    acc_s[...] = jnp.zeros(acc_s.shape, F32)
    for c in range(ATT_CHAINS):
        for h in range(HEADS):
            q_s[c, h * TQ:(h + 1) * TQ, :] = q_ref[c * TQ:(c + 1) * TQ, h * QH:(h + 1) * QH]

    def scores(c, s_ref, k0, size):
        k = kv_ref[pl.ds(k0, size), :]
        s_ref[c, :, :size] = lax.dot_general(q_s[c], k, (((1,), (1,)), ((), ())),
                                             preferred_element_type=F32)

    def softmax_pv(c, s_ref, k0, size):
        s = s_ref[c, :, :size]
        m_old = m_s[c]
        m_new = jnp.maximum(m_old, jnp.max(s, axis=1, keepdims=True))
        alpha = jnp.exp2(m_old - m_new)
        p = jnp.exp2(s - m_new).astype(BF16)
        acc_s[c] = alpha * acc_s[c] + jnp.dot(p, v1_ref[pl.ds(k0, size), :],
                                              preferred_element_type=F32)
        m_s[c] = m_new

    chains = range(ATT_CHAINS)

    @pl.when(i < n_ctx_qsteps)
    def _():
        for c in chains:
            scores(c, s_c, 0, n_ctx)
        for c in chains:
            softmax_pv(c, s_c, 0, n_ctx)

    tiles = [(s_c, 0, n_ctx)] + [((s_a, s_b)[t % 2], n_ctx + t * TK, TK) for t in range(n_lat // TK)]

    @pl.when(i >= n_ctx_qsteps)
    def _():
        for c in chains:
            scores(c, *tiles[0])
        for t, tile in enumerate(tiles):
            if t + 1 < len(tiles):
                for c in chains:
                    scores(c, *tiles[t + 1])
            for c in chains:
                softmax_pv(c, *tile)

    for c in chains:
        acc = acc_s[c]
        o = (acc[:, :KV_LORA] / acc[:, KV_LORA:KV_LORA + 1]).astype(BF16)
        for h in range(HEADS):
            o_ref[c * TQ:(c + 1) * TQ, h * KV_LORA:(h + 1) * KV_LORA] = o[h * TQ:(h + 1) * TQ, :]


def _attention(q, kv, v1, n_ctx):
    b, nt, _ = kv.shape
    rows = TQ * HEADS
    step = ATT_CHAINS * TQ
    n_lat = nt - n_ctx
    assert n_lat % TK == 0 and n_ctx % step == 0 and n_lat % step == 0
    kern = functools.partial(_attn_kernel, n_ctx=n_ctx, n_lat=n_lat, n_ctx_qsteps=n_ctx // step)
    return pl.pallas_call(
        kern,
        out_shape=jax.ShapeDtypeStruct((b, nt, HEADS * KV_LORA), BF16),
        grid=(b, nt // step),
        in_specs=[pl.BlockSpec((None, step, HEADS * QH), lambda bi, i: (bi, i, 0)),
                  pl.BlockSpec((None, nt, QH), lambda bi, i: (bi, 0, 0)),
                  pl.BlockSpec((None, nt, 2 * LANES), lambda bi, i: (bi, 0, 0))],
        out_specs=pl.BlockSpec((None, step, HEADS * KV_LORA), lambda bi, i: (bi, i, 0)),
        scratch_shapes=[pltpu.VMEM((ATT_CHAINS, rows, QH), BF16),
                        pltpu.VMEM((ATT_CHAINS, rows, 1), F32),
                        pltpu.VMEM((ATT_CHAINS, rows, 2 * LANES), F32),
                        pltpu.VMEM((ATT_CHAINS, rows, TK), F32),
                        pltpu.VMEM((ATT_CHAINS, rows, TK), F32),
                        pltpu.VMEM((ATT_CHAINS, rows, n_ctx), F32)],
        compiler_params=_cparams(("parallel", "arbitrary"), VMEM_LIMIT),
        name="mla_attention",
    )(q, kv, v1)


def _lru_tile(j, nct, ntl, reverse):
    if not reverse:
        return j
    return jnp.where(j < nct, nct - 1 - j, ntl - 1 - (j - nct))


def _lru_kernel(x_ref, prev_ref, next_ref, cw_ref, cb_ref, w_ref, ba_ref, bx_ref, lam_ref,
                o_ref, a_s, u_s, h_s, *, nct, ntl, reverse):
    j = pl.program_id(1)
    tile = _lru_tile(j, nct, ntl, reverse)

    @pl.when(j == 0)
    def _():
        h_s[...] = jnp.zeros(h_s.shape, F32)

    x = x_ref[...].astype(F32)
    has_prev = jnp.logical_and(tile != 0, tile != nct)
    has_next = jnp.logical_and(tile != nct - 1, tile != ntl - 1)
    prev = jnp.where(has_prev, prev_ref[...].astype(F32), 0.0)
    nxt = jnp.where(has_next, next_ref[...].astype(F32), 0.0)
    row = lax.broadcasted_iota(jnp.int32, (TM, D), 0)
    xm1 = jnp.where(row == 0, prev[HALO - 1:HALO, :], pltpu.roll(x, 1, 0))
    xm2 = jnp.where(row == 0, prev[HALO - 2:HALO - 1, :],
                    jnp.where(row == 1, prev[HALO - 1:HALO, :], pltpu.roll(x, 2, 0)))
    xp1 = jnp.where(row == TM - 1, nxt[0:1, :], pltpu.roll(x, TM - 1, 0))
    cw = cw_ref[...]
    xl = cb_ref[...] + xm2 * cw[0:1, :] + xm1 * cw[1:2, :] + x * cw[2:3, :] + xp1 * cw[3:4, :]

    z = -lam_ref[...]
    softplus = jnp.maximum(z, 0.0) + jnp.log1p(jnp.exp(-jnp.abs(z)))
    rate = -LRU_C * softplus
    for g in range(D // LANES):
        sl = slice(g * LANES, (g + 1) * LANES)
        xg = xl[:, sl]
        pre = jnp.dot(xg.astype(BF16), w_ref[g], preferred_element_type=F32)
        r = _sigmoid(pre[:, :LANES] + ba_ref[:, sl])
        ig = _sigmoid(pre[:, LANES:] + bx_ref[:, sl])
        log_a = rate[:, sl] * r
        a = jnp.exp(log_a)
        a_s[:, sl] = a
        u_s[:, sl] = jnp.sqrt(-jnp.tanh(log_a) * (1.0 + a * a)) * (ig * xg)

    srow = lax.broadcasted_iota(jnp.int32, (SUBLANES, D), 0)
    n_chunks = TM // SUBLANES

    def chunk(c, h_prev):
        cc = (n_chunks - 1 - c) if reverse else c
        r0 = pl.multiple_of(cc * SUBLANES, SUBLANES)
        a = a_s[pl.ds(r0, SUBLANES), :]
        u = u_s[pl.ds(r0, SUBLANES), :]
        for s in (1, 2, 4):
            if reverse:
                keep = srow < SUBLANES - s
                shift = SUBLANES - s
            else:
                keep = srow >= s
                shift = s
            u = u + a * jnp.where(keep, pltpu.roll(u, shift, 0), 0.0)
            a = a * jnp.where(keep, pltpu.roll(a, shift, 0), 1.0)
        h = u + a * h_prev
        o_ref[pl.ds(r0, SUBLANES), :] = h
        edge = h[0:1, :] if reverse else h[SUBLANES - 1:SUBLANES, :]
        return jnp.broadcast_to(edge, (SUBLANES, D))

    h_s[...] = lax.fori_loop(0, n_chunks, chunk, h_s[...])


def _lru(zb, cw, cb, wd, ba, bx, lam, nct, reverse):
    b, nt, _ = zb.shape
    ntl = nt // TM
    hb = TM // HALO
    n_hb = nt // HALO
    tile = lambda j: _lru_tile(j, nct, ntl, reverse)
    kern = functools.partial(_lru_kernel, nct=nct, ntl=ntl, reverse=reverse)
    vec = lambda n: pl.BlockSpec((n, D), lambda bi, j: (0, 0))
    return pl.pallas_call(
        kern,
        out_shape=jax.ShapeDtypeStruct((b, nt, D), F32),
        grid=(b, ntl),
        in_specs=[pl.BlockSpec((None, TM, D), lambda bi, j: (bi, tile(j), 0)),
                  pl.BlockSpec((None, HALO, D),
                               lambda bi, j: (bi, jnp.maximum(tile(j) * hb - 1, 0), 0)),
                  pl.BlockSpec((None, HALO, D),
                               lambda bi, j: (bi, jnp.minimum((tile(j) + 1) * hb, n_hb - 1), 0)),
                  vec(CONV_W), vec(1),
                  pl.BlockSpec((D // LANES, LANES, 2 * LANES), lambda bi, j: (0, 0, 0)),
                  vec(1), vec(1), vec(1)],
        out_specs=pl.BlockSpec((None, TM, D), lambda bi, j: (bi, tile(j), 0)),
        scratch_shapes=[pltpu.VMEM((TM, D), F32), pltpu.VMEM((TM, D), F32),
                        pltpu.VMEM((SUBLANES, D), F32)],
        compiler_params=_cparams(("arbitrary", "arbitrary"), VMEM_LIMIT),
        name="rglru_bwd" if reverse else "rglru_fwd",
    )(zb, zb, zb, cw, cb, wd, ba, bx, lam)


def _merge_kernel(o_ref, hf_ref, hb_ref, gr_ref, ga_ref, gl_ref, x_ref, mod_ref, bg_ref, nf_ref,
                  wvo_ref, wol_ref, wout_ref, wr_ref, br_ref,
                  xn_ref, h2_ref, rg_ref, rt_ref, meta_ref, cnt_ref, carry_s):
    first = jnp.logical_and(pl.program_id(0) == 0, pl.program_id(1) == 0)

    @pl.when(first)
    def _():
        carry_s[...] = jnp.zeros(carry_s.shape, F32)

    wr = wr_ref[...]
    w_hi = wr.astype(BF16)
    w_lo = (wr - w_hi.astype(F32)).astype(BF16)

    gr = gr_ref[...].astype(F32)
    gelu = 0.5 * gr * (1.0 + jnp.tanh(math.sqrt(2.0 / math.pi) * (gr + 0.044715 * (gr * gr * gr))))
    lru = ((hf_ref[...] + hb_ref[...]) * gelu).astype(BF16)
    y_att = jnp.dot(o_ref[...], wvo_ref[...], preferred_element_type=F32)
    y_lru = jnp.dot(lru, wol_ref[...], preferred_element_type=F32)
    m = (_sigmoid(ga_ref[...].astype(F32) + bg_ref[0]) * y_att
         + _sigmoid(gl_ref[...].astype(F32) + bg_ref[1]) * y_lru)
    out = jnp.dot(m.astype(BF16), wout_ref[...], preferred_element_type=F32)
    xn = x_ref[...] + mod_ref[2] * out
    xn_ref[...] = xn
    h2 = _rms(xn, nf_ref[...]) * (1.0 + mod_ref[4]) + mod_ref[3]
    h_hi = h2.astype(BF16)
    h2_ref[...] = h_hi
    h_lo = (h2 - h_hi.astype(F32)).astype(BF16)
    logits = (jnp.dot(h_hi, w_hi, preferred_element_type=F32)
              + (jnp.dot(h_lo, w_hi, preferred_element_type=F32)
                 + jnp.dot(h_hi, w_lo, preferred_element_type=F32))) + br_ref[...]
    lane = lax.broadcasted_iota(jnp.int32, (TM, LANES), 1)
    lane_f = lane.astype(F32)
    l = logits
    vals, idxs = [], []
    for _ in range(TOP_K):
        mx = jnp.max(l, axis=1, keepdims=True)
        ix = jnp.min(jnp.where(l == mx, lane_f, float(LANES)), axis=1, keepdims=True)
        vals.append(mx)
        idxs.append(ix)
        l = jnp.where(lane_f == ix, NEG_INF, l)
    chosen = jnp.where(jnp.logical_and(l == NEG_INF, lane < N_EXPERTS), 1.0, 0.0)
    es = [jnp.exp(v - vals[0]) for v in vals]
    den = es[0] + es[1] + es[2] + es[3]

    r_i = lax.broadcasted_iota(jnp.int32, (TM, TM), 0)
    c_i = lax.broadcasted_iota(jnp.int32, (TM, TM), 1)
    tri = jnp.where(r_i > c_i, 1.0, 0.0).astype(BF16)
    before = jnp.dot(tri, chosen.astype(BF16), preferred_element_type=F32)
    cnt_t = jnp.sum(chosen, axis=0, keepdims=True)
    seg_len = jnp.floor((cnt_t + (MOE_CH - 1.0)) * (1.0 / MOE_CH)) * MOE_CH
    e_r = lax.broadcasted_iota(jnp.int32, (LANES, LANES), 0)
    e_c = lax.broadcasted_iota(jnp.int32, (LANES, LANES), 1)
    upper = jnp.where(e_r < e_c, 1.0, 0.0).astype(BF16)
    seg_off = jnp.dot(jnp.broadcast_to(seg_len, (SUBLANES, LANES)).astype(BF16), upper,
                      preferred_element_type=F32)[0:1, :]
    slot_of = before + seg_off
    rg = jnp.zeros((TM, LANES), F32)
    for k in range(TOP_K):
        slot = jnp.sum(jnp.where(lane_f == idxs[k], slot_of, 0.0), axis=1, keepdims=True)
        rg = jnp.where(lane == k, es[k] / den, rg)
        rg = jnp.where(lane == TOP_K + k, slot, rg)
    rg_ref[...] = rg
    rt_ref[...] = jnp.transpose(rg)[0:SUBLANES, :]
    row8 = lax.broadcasted_iota(jnp.int32, (SUBLANES, LANES), 0)
    carry = carry_s[...]
    meta_ref[...] = jnp.where(row8 == 0, cnt_t, jnp.where(row8 == 1, carry, 0.0)).astype(jnp.int32)
    carry = carry + jnp.floor((cnt_t + (SUBLANES - 1.0)) * (1.0 / SUBLANES)) * SUBLANES
    carry_s[...] = carry
    cnt_ref[...] = jnp.broadcast_to(carry, cnt_ref.shape).astype(jnp.int32)


def _merge(o, hf, hb, zb, x, mods, bg, nf, wvo, wol, wout, wr, br, nct, tile0):
    b, nt, _ = x.shape
    ntl = nt // TM
    tok = lambda c: pl.BlockSpec((None, TM, D), lambda bi, j: (bi, j + tile0, c))
    full = lambda *s: pl.BlockSpec(s, lambda bi, j: (0,) * len(s))
    out_nt = nt - tile0 * TM
    return pl.pallas_call(
        _merge_kernel,
        out_shape=(jax.ShapeDtypeStruct((b, out_nt, D), F32),
                   jax.ShapeDtypeStruct((b, out_nt, D), BF16),
                   jax.ShapeDtypeStruct((b, out_nt, LANES), F32),
                   jax.ShapeDtypeStruct((b, out_nt // TM, SUBLANES, TM), F32),
                   jax.ShapeDtypeStruct((b, out_nt // TM, SUBLANES, LANES), jnp.int32),
                   jax.ShapeDtypeStruct((SUBLANES, LANES), jnp.int32)),
        grid=(b, ntl - tile0),
        in_specs=[tok(0), tok(0), tok(0), tok(1), tok(2), tok(3), tok(0),
                  pl.BlockSpec((None, None, 6, 1, D),
                               lambda bi, j: (bi, _seg(j + tile0, nct), 0, 0, 0)),
                  full(2, 1, D), full(1, D), full(D, D), full(D, D), full(D, D),
                  full(D, LANES), full(1, LANES)],
        out_specs=(pl.BlockSpec((None, TM, D), lambda bi, j: (bi, j, 0)),
                   pl.BlockSpec((None, TM, D), lambda bi, j: (bi, j, 0)),
                   pl.BlockSpec((None, TM, LANES), lambda bi, j: (bi, j, 0)),
                   pl.BlockSpec((None, None, SUBLANES, TM), lambda bi, j: (bi, j, 0, 0)),
                   pl.BlockSpec((None, None, SUBLANES, LANES), lambda bi, j: (bi, j, 0, 0)),
                   pl.BlockSpec((SUBLANES, LANES), lambda bi, j: (0, 0))),
        scratch_shapes=[pltpu.VMEM((1, LANES), F32)],
        compiler_params=_cparams(("arbitrary", "arbitrary"), VMEM_LIMIT),
        name="merge_route",
    )(o, hf, hb, zb, zb, zb, x, mods, bg, nf, wvo, wol, wout, wr, br)


def _dispatch_kernel(meta_ref, h2_ref, rt_ref, xs_ref, sorted_s, zero_s, sem, *, n_blk, ntl):
    bi = pl.program_id(0)
    j = pl.program_id(1)
    first = jnp.logical_and(bi == 0, j == 0)

    @pl.when(first)
    def _():
        zero_s[...] = jnp.zeros(zero_s.shape, F32)

        def pad_copy(i):
            r0 = pl.multiple_of(i * SUBLANES, SUBLANES)
            return pltpu.make_async_copy(zero_s.at[pl.ds(0, SUBLANES), :],
                                         xs_ref.at[pl.ds(r0, SUBLANES), :], sem)

        def blk_copy(i):
            r0 = pl.multiple_of(i * MOE_R, MOE_R)
            return pltpu.make_async_copy(zero_s, xs_ref.at[pl.ds(r0, MOE_R), :], sem)

        def blk_issue(i, c):
            blk_copy(i).start()
            return c

        def blk_drain(i, c):
            blk_copy(i).wait()
            return c
        n_used = meta_ref[META_NUSED]
        lax.fori_loop(n_used, n_blk, blk_issue, 0)
        lax.fori_loop(n_used, n_blk, blk_drain, 0)

        def per_expert(e, c):
            lo = meta_ref[META_TAIL + e] // SUBLANES
            hi = meta_ref[META_END + e] // SUBLANES

            def issue(r, c2):
                pad_copy(r).start()
                return c2

            def drain(r, c2):
                pad_copy(r).wait()
                return c2
            lax.fori_loop(lo, hi, issue, 0)
            lax.fori_loop(lo, hi, drain, 0)
            return c
        lax.fori_loop(0, N_EXPERTS, per_expert, 0)

    slot_i = lax.broadcasted_iota(jnp.int32, (MOE_SLOTS, TM), 0).astype(F32)
    hit = slot_i == rt_ref[TOP_K:TOP_K + 1, :]
    for k in range(1, TOP_K):
        hit = jnp.logical_or(hit, slot_i == rt_ref[TOP_K + k:TOP_K + k + 1, :])
    onehot = jnp.where(hit, 1.0, 0.0).astype(BF16)
    tile = bi * ntl + j
    cur = tile % 2
    sorted_s[cur] = jnp.dot(onehot, h2_ref[...], preferred_element_type=F32)

    def refs_of(buf):
        return lambda slot, row: (sorted_s.at[buf, pl.ds(slot, MOE_CH), :],
                                  xs_ref.at[pl.ds(row, MOE_CH), :])

    @pl.when(tile > 0)
    def _():
        _moe_chunk_copies(meta_ref, tile - 1, sem, refs_of(1 - cur), start=False)
    _moe_chunk_copies(meta_ref, tile, sem, refs_of(cur), start=True)

    @pl.when(tile == pl.num_programs(0) * ntl - 1)
    def _():
        _moe_chunk_copies(meta_ref, tile, sem, refs_of(cur), start=False)


def _moe_chunk_copies(meta_ref, tile, sem, refs, *, start):
    base = META_TILES + tile * META_PER_TILE

    def piece(c, carry):
        src, dst = refs(pl.multiple_of(meta_ref[base + 1 + 2 * c], MOE_CH),
                        pl.multiple_of(meta_ref[base + 2 + 2 * c], SUBLANES))
        cp = pltpu.make_async_copy(src, dst, sem)
        if start:
            cp.start()
        else:
            cp.wait()
        return carry
    lax.fori_loop(0, meta_ref[base], piece, 0)


def _dispatch(meta, h2, rt, n_rows):
    b, nt, _ = h2.shape
    ntl = nt // TM
    kern = functools.partial(_dispatch_kernel, n_blk=n_rows // MOE_R, ntl=ntl)
    return pl.pallas_call(
        kern,
        out_shape=jax.ShapeDtypeStruct((n_rows, D), F32),
        grid_spec=pltpu.PrefetchScalarGridSpec(
            num_scalar_prefetch=1,
            grid=(b, ntl),
            in_specs=[pl.BlockSpec((None, TM, D), lambda bi, j, meta: (bi, j, 0)),
                      pl.BlockSpec((None, None, SUBLANES, TM), lambda bi, j, meta: (bi, j, 0, 0))],
            out_specs=pl.BlockSpec(memory_space=pl.ANY),
            scratch_shapes=[pltpu.VMEM((2, MOE_SLOTS, D), F32),
                            pltpu.VMEM((MOE_R, D), F32),
                            pltpu.SemaphoreType.DMA]),
        compiler_params=_cparams(("arbitrary", "arbitrary"), VMEM_LIMIT),
        name="moe_dispatch",
    )(meta, h2, rt)


def _expert_kernel(be_ref, nu_ref, x_ref, wg_ref, bg_ref, wu_ref, bu_ref, wd_ref, bd_ref,
                   y_ref, wg_s, wu_s, wd_s):
    blk = pl.program_id(0)
    used = blk < nu_ref[0]
    prev = be_ref[jnp.maximum(blk - 1, 0)]
    fresh = jnp.logical_or(blk == 0, be_ref[blk] != prev)

    @pl.when(jnp.logical_and(used, fresh))
    def _():
        wg_s[...] = wg_ref[...].astype(BF16)
        wu_s[...] = wu_ref[...].astype(BF16)
        wd_s[...] = wd_ref[...].astype(BF16)

    @pl.when(used)
    def _():
        x = x_ref[...].astype(BF16)
        g = jnp.dot(x, wg_s[...], preferred_element_type=F32) + bg_ref[...]
        u = jnp.dot(x, wu_s[...], preferred_element_type=F32) + bu_ref[...]
        g = jnp.minimum(g, SWIGLU_LIMIT)
        u = jnp.clip(u, -SWIGLU_LIMIT, SWIGLU_LIMIT)
        act = g * _sigmoid(SWIGLU_ALPHA * g) * (u + 1.0)
        y_ref[...] = jnp.dot(act.astype(BF16), wd_s[...], preferred_element_type=F32) + bd_ref[...]

    @pl.when(jnp.logical_not(used))
    def _():
        y_ref[...] = jnp.zeros(y_ref.shape, F32)


def _experts(blk_expert, n_used, xs, wg, bg, wu, bu, wd, bd, layer):
    n_rows = xs.shape[0]
    n_blk = n_rows // MOE_R
    dff = wg.shape[-1]
    e_of = lambda i, be, nu: be[jnp.minimum(i, nu[0] - 1)]
    wspec = lambda r, c: pl.BlockSpec((None, None, r, c), lambda i, be, nu: (layer, e_of(i, be, nu), 0, 0))
    return pl.pallas_call(
        _expert_kernel,
        out_shape=jax.ShapeDtypeStruct((n_rows, D), F32),
        grid_spec=pltpu.PrefetchScalarGridSpec(
            num_scalar_prefetch=2,
            grid=(n_blk,),
            in_specs=[pl.BlockSpec((MOE_R, D), lambda i, be, nu: (jnp.minimum(i, nu[0] - 1), 0)),
                      wspec(D, dff), wspec(1, dff), wspec(D, dff), wspec(1, dff),
                      wspec(dff, D), wspec(1, D)],
            out_specs=pl.BlockSpec((MOE_R, D), lambda i, be, nu: (i, 0)),
            scratch_shapes=[pltpu.VMEM((D, dff), BF16), pltpu.VMEM((D, dff), BF16),
                            pltpu.VMEM((dff, D), BF16)]),
        compiler_params=_cparams(("arbitrary",), VMEM_LIMIT),
        name="moe_experts",
    )(blk_expert, n_used, xs, wg, bg, wu, bu, wd, bd)


def _combine_kernel(meta_ref, x_ref, rg_ref, mod_ref, fn_ref, ys_ref, o_ref, buf_s, sem, *, final, ntl):
    bi = pl.program_id(0)
    j = pl.program_id(1)

    tile = bi * ntl + j
    cur = tile % 2

    def refs_of(buf):
        return lambda slot, row: (ys_ref.at[pl.ds(row, MOE_CH), :],
                                  buf_s.at[buf, pl.ds(slot, MOE_CH), :])

    @pl.when(tile == 0)
    def _():
        buf_s[...] = jnp.zeros(buf_s.shape, F32)
        _moe_chunk_copies(meta_ref, tile, sem, refs_of(cur), start=True)

    _moe_chunk_copies(meta_ref, tile, sem, refs_of(cur), start=False)

    @pl.when(tile < pl.num_programs(0) * ntl - 1)
    def _():
        _moe_chunk_copies(meta_ref, tile + 1, sem, refs_of(1 - cur), start=True)

    rg = rg_ref[...]
    slot_i = lax.broadcasted_iota(jnp.int32, (TM, MOE_SLOTS), 1).astype(F32)
    gate = jnp.zeros((TM, MOE_SLOTS), F32)
    for k in range(TOP_K):
        gate = jnp.where(slot_i == rg[:, TOP_K + k:TOP_K + k + 1], rg[:, k:k + 1], gate)
    y = jnp.dot(gate.astype(BF16), buf_s[cur].astype(BF16), preferred_element_type=F32)
    xo = x_ref[...] + mod_ref[5] * y
    if final:
        xo = _rms(xo, fn_ref[...])
    o_ref[...] = xo


def _combine(meta, x, rg, mods, fn, ys, nct, final):
    b, nt, _ = x.shape
    ntl = nt // TM
    kern = functools.partial(_combine_kernel, final=final, ntl=ntl)
    return pl.pallas_call(
        kern,
        out_shape=jax.ShapeDtypeStruct((b, nt, D), F32),
        grid_spec=pltpu.PrefetchScalarGridSpec(
            num_scalar_prefetch=1,
            grid=(b, ntl),
            in_specs=[pl.BlockSpec((None, TM, D), lambda bi, j, meta: (bi, j, 0)),
                      pl.BlockSpec((None, TM, LANES), lambda bi, j, meta: (bi, j, 0)),
                      pl.BlockSpec((None, None, 6, 1, D),
                                   lambda bi, j, meta: (bi, _seg(j, nct), 0, 0, 0)),
                      pl.BlockSpec((1, D), lambda bi, j, meta: (0, 0)),
                      pl.BlockSpec(memory_space=pl.ANY)],
            out_specs=pl.BlockSpec((None, TM, D), lambda bi, j, meta: (bi, j, 0)),
            scratch_shapes=[pltpu.VMEM((2, MOE_SLOTS, D), F32), pltpu.SemaphoreType.DMA]),
        compiler_params=_cparams(("arbitrary", "arbitrary"), VMEM_LIMIT),
        name="moe_combine",
    )(meta, x, rg, mods, fn, ys)


_ROT_PARTNER = tuple(list(range(16, 32)) + list(range(0, 16)) + list(range(48, 64)) + list(range(32, 48)))


def _rope_tables(n_ctx, n_lat):
    n = jnp.arange(n_lat)
    row = (n // GRID_W).astype(F32)
    col = (n % GRID_W).astype(F32)
    n_freq = QK_ROPE // 4
    inv = ROPE_THETA ** (-jnp.arange(n_freq, dtype=F32) / n_freq)
    ar = row[:, None] * inv
    ac = col[:, None] * inv
    c64 = jnp.concatenate([jnp.cos(ar), jnp.cos(ar), jnp.cos(ac), jnp.cos(ac)], axis=1)
    s64 = jnp.concatenate([-jnp.sin(ar), jnp.sin(ar), -jnp.sin(ac), jnp.sin(ac)], axis=1)
    c64 = jnp.concatenate([jnp.ones((n_ctx, QK_ROPE), F32), c64], axis=0)
    s64 = jnp.concatenate([jnp.zeros((n_ctx, QK_ROPE), F32), s64], axis=0)
    nt = n_ctx + n_lat
    qs = ATTN_SCALE * LOG2E
    cq = qs * jnp.concatenate([jnp.ones((nt, QK_NOPE), F32), c64, c64], axis=1)
    sq = qs * jnp.concatenate([jnp.zeros((nt, QK_NOPE), F32), s64, s64], axis=1)
    ck = jnp.concatenate([c64, s64], axis=1)
    return cq, sq, ck


def _block_diag_pairs(w):
    w = w.reshape(LRU_BLOCKS // 2, 2, LRU_BW, LRU_BW)
    z = jnp.zeros((LRU_BLOCKS // 2, LRU_BW, LRU_BW), w.dtype)
    top = jnp.concatenate([w[:, 0], z], axis=2)
    bot = jnp.concatenate([z, w[:, 1]], axis=2)
    return jnp.concatenate([top, bot], axis=1)


def kernel(x, c, ctx, c_ctx, w_ada, b_ada, norm_mix, norm_ffn, w_in, b_branch_gate, q_norm, w_uq, kv_norm, w_ukv, w_o_attn, conv_w, conv_b, lru_w_a, lru_b_a, lru_w_x, lru_b_x, lru_lambda, w_o_lru, w_out, w_router, b_router, w_exp_gate, b_exp_gate, w_exp_up, b_exp_up, w_exp_down, b_exp_down, final_norm):
    depth = w_in.shape[0]
    b, n_lat, _ = x.shape
    n_ctx = ctx.shape[1]
    assert n_ctx % TM == 0 and n_lat % TK == 0 and b + 1 <= SUBLANES
    nct = n_ctx // TM
    nt = n_ctx + n_lat
    ntl = nt // TM
    perm = jnp.array(_ROT_PARTNER)

    xs = jnp.concatenate([ctx, x], axis=1)
    cond8 = jnp.zeros((SUBLANES, D), F32).at[:b].set(c).at[b].set(c_ctx)
    cq_t, sq_t, ck_t = _rope_tables(n_ctx, n_lat)

    for l in range(depth):
        last = l == depth - 1
        tile0 = nct if last else 0

        m8 = _ada(cond8, w_ada[l], b_ada[l][None, :]).reshape(SUBLANES, 6, 1, D)
        mods = jnp.stack([jnp.broadcast_to(m8[b], (b,) + m8.shape[1:]), m8[:b]], axis=1)

        wi = w_in[l]
        kpe_w = wi[:, Q_LORA + KV_LORA:Q_LORA + KV_LORA + QK_ROPE]
        wa = jnp.concatenate([wi[:, :Q_LORA + KV_LORA + QK_ROPE], kpe_w[:, perm]], axis=1).astype(BF16)
        wb = wi[:, Q_LORA + KV_LORA + QK_ROPE:].astype(BF16)
        uq = w_uq[l].reshape(Q_LORA, HEADS, QK_DIM).transpose(1, 0, 2)
        ukv = w_ukv[l].reshape(KV_LORA, HEADS, QK_NOPE + V_HEAD).transpose(1, 0, 2)
        wqa, wvo = _fold(uq[:, :, :QK_NOPE], ukv[:, :, :QK_NOPE], ukv[:, :, QK_NOPE:],
                         w_o_attn[l].reshape(HEADS, V_HEAD, D))
        pe = uq[:, :, QK_NOPE:]
        pe_sw = pe[:, :, perm]
        zq = jnp.zeros((HEADS, Q_LORA, QK_NOPE), F32)
        wq1 = jnp.concatenate([wqa, pe, pe], axis=2)
        wq2 = jnp.concatenate([zq, pe_sw, pe_sw], axis=2)
        wq = jnp.concatenate([wq1.transpose(1, 0, 2).reshape(Q_LORA, HEADS * QH),
                              wq2.transpose(1, 0, 2).reshape(Q_LORA, HEADS * QH)], axis=1).astype(BF16)
        wvo = wvo.reshape(D, D).astype(BF16)
        wr = jnp.zeros((D, LANES), F32).at[:, :N_EXPERTS].set(w_router[l])
        br = jnp.full((1, LANES), NEG_INF, F32).at[0, :N_EXPERTS].set(b_router[l])

        za, zb = _inproj(xs, mods, norm_mix[l][None, :], wa, wb, nct)
        q, kv, v1 = _mlaproj(za, q_norm[l][None, :], kv_norm[l][None, :], wq, cq_t, sq_t, ck_t)
        o = _attention(q, kv, v1, n_ctx)
        hs = []
        for d in range(2):
            wd = jnp.concatenate([_block_diag_pairs(lru_w_a[l, d]), _block_diag_pairs(lru_w_x[l, d])],
                                 axis=2).astype(BF16)
            hs.append(_lru(zb, conv_w[l], conv_b[l][None, :], wd, lru_b_a[l, d][None, :],
                           lru_b_x[l, d][None, :], lru_lambda[l, d][None, :], nct, reverse=d == 1))
        xn, h2, rg, rt, tile_meta, cnt = _merge(
            o, hs[0], hs[1], zb, xs, mods, b_branch_gate[l].reshape(2, 1, D), norm_ffn[l][None, :],
            wvo, w_o_lru[l].astype(BF16), w_out[l].astype(BF16), wr, br, nct, tile0)

        counts = cnt[0, :N_EXPERTS]
        padded = ((counts + (MOE_CH - 1) + MOE_R - 1) // MOE_R) * MOE_R
        pad_end = jnp.cumsum(padded)
        pad_start = pad_end - padded
        moe_nt = xn.shape[1]
        n_tiles = b * moe_nt // TM
        max_rows = (b * moe_nt * TOP_K + n_tiles * N_EXPERTS * (SUBLANES - 1)
                    + N_EXPERTS * (MOE_CH - 1))
        n_blk = max_rows // MOE_R + N_EXPERTS
        n_rows = n_blk * MOE_R
        blk_start = jnp.arange(n_blk, dtype=jnp.int32) * MOE_R
        blk_expert = jnp.minimum(jnp.sum(blk_start[:, None] >= pad_end[None, :], axis=1),
                                 N_EXPERTS - 1).astype(jnp.int32)
        n_used = (pad_end[-1:] // MOE_R).astype(jnp.int32)
        cnt_t = tile_meta[:, :, 0, :N_EXPERTS].reshape(n_tiles, N_EXPERTS)
        row0_t = pad_start[None, :] + tile_meta[:, :, 1, :N_EXPERTS].reshape(n_tiles, N_EXPERTS)
        n_ch = (cnt_t + MOE_CH - 1) // MOE_CH
        ch_end = jnp.cumsum(n_ch, axis=1)
        ch_start = ch_end - n_ch
        c_ids = jnp.arange(MOE_MAX_COPIES, dtype=jnp.int32)
        e_of_c = jnp.minimum(jnp.sum(c_ids[None, :, None] >= ch_end[:, None, :], axis=-1), N_EXPERTS - 1)
        is_e = e_of_c[:, :, None] == jnp.arange(N_EXPERTS, dtype=jnp.int32)
        pick = lambda v: jnp.sum(jnp.where(is_e, v[:, None, :], 0), axis=-1)
        first_c = pick(ch_start)
        piece = (c_ids[None, :] - first_c) * MOE_CH
        copies = jnp.stack([first_c * MOE_CH + piece, pick(row0_t) + piece], axis=-1)
        tile_tab = jnp.concatenate(
            [ch_end[:, -1:], copies.reshape(n_tiles, 2 * MOE_MAX_COPIES),
             jnp.zeros((n_tiles, META_PER_TILE - 1 - 2 * MOE_MAX_COPIES), jnp.int32)], axis=1)
        meta = jnp.concatenate([pad_start, pad_start + counts, pad_end,
                                jnp.broadcast_to(n_used, (N_EXPERTS,)),
                                tile_tab.reshape(-1)]).astype(jnp.int32)

        xsort = _dispatch(meta, h2, rt, n_rows)
        ys = _experts(blk_expert, n_used, xsort, w_exp_gate, b_exp_gate[:, :, None, :],
                      w_exp_up, b_exp_up[:, :, None, :], w_exp_down, b_exp_down[:, :, None, :], l)
        xs = _combine(meta, xn, rg, mods, final_norm[None, :], ys, nct - tile0, last)
    return xs
```

```python
import functools
import math

import jax
import jax.numpy as jnp
from jax import lax
from jax.experimental import pallas as pl
from jax.experimental.pallas import tpu as pltpu

D = 1024
GRID_W = 64
HEADS = 8
QK_NOPE = 128
QK_ROPE = 64
QK_DIM = QK_NOPE + QK_ROPE
V_HEAD = 128
Q_LORA = 256
KV_LORA = 128
ROPE_THETA = 10000.0
ATTN_SCALE = QK_DIM ** -0.5
LRU_BLOCKS = 16
LRU_BW = D // LRU_BLOCKS
CONV_W = 4
LRU_C = 8.0
N_EXPERTS = 32
TOP_K = 4
SWIGLU_ALPHA = 1.702
SWIGLU_LIMIT = 7.0
NORM_EPS = 1e-6
D_ZA = Q_LORA + KV_LORA + 2 * QK_ROPE
D_ZB = 4 * D
QH = 256

LANES = 128
SUBLANES = 8
TM = 256
HALO = 16
TQ = 128
ATT_CHAINS = 1
TK = 2048
MOE_R = 256
MOE_CH = 16
MOE_SLOTS = TM * TOP_K + N_EXPERTS * MOE_CH
META_TAIL = N_EXPERTS
META_END = 2 * N_EXPERTS
META_NUSED = 3 * N_EXPERTS
META_TILES = 4 * N_EXPERTS
MOE_MAX_COPIES = MOE_SLOTS // MOE_CH
META_PER_TILE = 256
assert 1 + 2 * MOE_MAX_COPIES <= META_PER_TILE
VMEM_LIMIT = 56 * 1024 * 1024

F32 = jnp.float32
BF16 = jnp.bfloat16
NEG_INF = float("-inf")
LOG2E = 1.4426950408889634


def _cparams(sem, vmem=None):
    return pltpu.CompilerParams(dimension_semantics=sem, vmem_limit_bytes=vmem)


def _rms(x, g):
    return x * lax.rsqrt(jnp.mean(x * x, axis=-1, keepdims=True) + NORM_EPS) * g


def _sigmoid(x):
    return 0.5 * jnp.tanh(0.5 * x) + 0.5


def _seg(i, nct):
    return jnp.where(i < nct, 0, 1)


def _ada_kernel(c_ref, w_ref, b_ref, o_ref):
    c = c_ref[...]
    s = (c * _sigmoid(c)).astype(BF16)
    o_ref[...] = jnp.dot(s, w_ref[...].astype(BF16), preferred_element_type=F32) + b_ref[...]


def _ada(cond8, w, b, tn=768):
    n = w.shape[1]
    return pl.pallas_call(
        _ada_kernel,
        out_shape=jax.ShapeDtypeStruct((SUBLANES, n), F32),
        grid=(n // tn,),
        in_specs=[pl.BlockSpec((SUBLANES, D), lambda j: (0, 0)),
                  pl.BlockSpec((D, tn), lambda j: (0, j)),
                  pl.BlockSpec((1, tn), lambda j: (0, j))],
        out_specs=pl.BlockSpec((SUBLANES, tn), lambda j: (0, j)),
        compiler_params=_cparams(("parallel",)),
        name="adaln",
    )(cond8, w, b)


def _fold_kernel(nope_ref, wuk_ref, wuv_ref, wo_ref, wqa_ref, wvo_ref):
    hp = lax.Precision.HIGHEST
    wqa_ref[...] = lax.dot_general(nope_ref[...], wuk_ref[...], (((1,), (1,)), ((), ())),
                                   precision=hp, preferred_element_type=F32)
    wvo_ref[...] = jnp.dot(wuv_ref[...], wo_ref[...], precision=hp, preferred_element_type=F32)


def _fold(nope, wuk, wuv, wo):
    sq = lambda *s: pl.BlockSpec((None,) + s, lambda h: (h, 0, 0))
    return pl.pallas_call(
        _fold_kernel,
        out_shape=(jax.ShapeDtypeStruct((HEADS, Q_LORA, KV_LORA), F32),
                   jax.ShapeDtypeStruct((HEADS, KV_LORA, D), F32)),
        grid=(HEADS,),
        in_specs=[sq(Q_LORA, QK_NOPE), sq(KV_LORA, QK_NOPE), sq(KV_LORA, V_HEAD), sq(V_HEAD, D)],
        out_specs=(sq(Q_LORA, KV_LORA), sq(KV_LORA, D)),
        compiler_params=_cparams(("parallel",)),
        name="mla_fold",
    )(nope, wuk, wuv, wo)


def _inproj_kernel(x_ref, mod_ref, g_ref, wa_ref, wb_ref, za_ref, zb_ref):
    h = _rms(x_ref[...], g_ref[...]) * (1.0 + mod_ref[1]) + mod_ref[0]
    hb = h.astype(BF16)
    za_ref[...] = jnp.dot(hb, wa_ref[...], preferred_element_type=F32)
    for j in range(D_ZB // D):
        zb_ref[:, j * D:(j + 1) * D] = jnp.dot(
            hb, wb_ref[:, j * D:(j + 1) * D], preferred_element_type=F32).astype(BF16)


def _inproj(x, mods, g, wa, wb, nct):
    b, nt, _ = x.shape
    return pl.pallas_call(
        _inproj_kernel,
        out_shape=(jax.ShapeDtypeStruct((b, nt, D_ZA), F32),
                   jax.ShapeDtypeStruct((b, nt, D_ZB), BF16)),
        grid=(b, nt // TM),
        in_specs=[pl.BlockSpec((None, TM, D), lambda bi, i: (bi, i, 0)),
                  pl.BlockSpec((None, None, 6, 1, D), lambda bi, i: (bi, _seg(i, nct), 0, 0, 0)),
                  pl.BlockSpec((1, D), lambda bi, i: (0, 0)),
                  pl.BlockSpec((D, D_ZA), lambda bi, i: (0, 0)),
                  pl.BlockSpec((D, D_ZB), lambda bi, i: (0, 0))],
        out_specs=(pl.BlockSpec((None, TM, D_ZA), lambda bi, i: (bi, i, 0)),
                   pl.BlockSpec((None, TM, D_ZB), lambda bi, i: (bi, i, 0))),
        compiler_params=_cparams(("parallel", "parallel"), VMEM_LIMIT),
        name="in_proj",
    )(x, mods, g, wa, wb)


def _mlaproj_kernel(za_ref, qn_ref, kvn_ref, wq_ref, cq_ref, sq_ref, ck_ref, q_ref, kv_ref, v1_ref):
    za = za_ref[...]
    cq = _rms(za[:, :Q_LORA], qn_ref[...]).astype(BF16)
    r = jnp.dot(cq, wq_ref[...], preferred_element_type=F32)
    cqt = cq_ref[...]
    sqt = sq_ref[...]
    for h in range(HEADS):
        a = r[:, h * QH:(h + 1) * QH]
        s = r[:, (HEADS + h) * QH:(HEADS + h + 1) * QH]
        q_ref[:, h * QH:(h + 1) * QH] = (a * cqt + s * sqt).astype(BF16)
    ckv = _rms(za[:, Q_LORA:Q_LORA + KV_LORA], kvn_ref[...])
    kv_ref[:, :KV_LORA] = ckv.astype(BF16)
    kv_ref[:, KV_LORA:] = (za[:, Q_LORA + KV_LORA:] * ck_ref[...]).astype(BF16)
    lane = lax.broadcasted_iota(jnp.int32, (TM, LANES), 1)
    v1_ref[:, :KV_LORA] = ckv.astype(BF16)
    v1_ref[:, KV_LORA:] = jnp.where(lane == 0, 1.0, 0.0).astype(BF16)


def _mlaproj(za, qn, kvn, wq, cq, sq, ck):
    b, nt, _ = za.shape
    return pl.pallas_call(
        _mlaproj_kernel,
        out_shape=(jax.ShapeDtypeStruct((b, nt, HEADS * QH), BF16),
                   jax.ShapeDtypeStruct((b, nt, QH), BF16),
                   jax.ShapeDtypeStruct((b, nt, 2 * LANES), BF16)),
        grid=(b, nt // TM),
        in_specs=[pl.BlockSpec((None, TM, D_ZA), lambda bi, i: (bi, i, 0)),
                  pl.BlockSpec((1, Q_LORA), lambda bi, i: (0, 0)),
                  pl.BlockSpec((1, KV_LORA), lambda bi, i: (0, 0)),
                  pl.BlockSpec((Q_LORA, 2 * HEADS * QH), lambda bi, i: (0, 0)),
                  pl.BlockSpec((TM, QH), lambda bi, i: (i, 0)),
                  pl.BlockSpec((TM, QH), lambda bi, i: (i, 0)),
                  pl.BlockSpec((TM, 2 * QK_ROPE), lambda bi, i: (i, 0))],
        out_specs=(pl.BlockSpec((None, TM, HEADS * QH), lambda bi, i: (bi, i, 0)),
                   pl.BlockSpec((None, TM, QH), lambda bi, i: (bi, i, 0)),
                   pl.BlockSpec((None, TM, 2 * LANES), lambda bi, i: (bi, i, 0))),
        compiler_params=_cparams(("parallel", "parallel"), VMEM_LIMIT),
        name="mla_proj",
    )(za, qn, kvn, wq, cq, sq, ck)


def _attn_kernel(q_ref, kv_ref, v1_ref, o_ref, q_s, m_s, acc_s, s_a, s_b, s_c,
                 *, n_ctx, n_lat, n_ctx_qsteps):
    i = pl.program_id(1)
    m_s[...] = jnp.full(m_s.shape, NEG_INF, F32)
    acc_s[...] = jnp.zeros(acc_s.shape, F32)
    for c in range(ATT_CHAINS):
        for h in range(HEADS):
            q_s[c, h * TQ:(h + 1) * TQ, :] = q_ref[c * TQ:(c + 1) * TQ, h * QH:(h + 1) * QH]

    def scores(c, s_ref, k0, size):
        k = kv_ref[pl.ds(k0, size), :]
        s_ref[c, :, :size] = lax.dot_general(q_s[c], k, (((1,), (1,)), ((), ())),
                                             preferred_element_type=F32)

    def softmax_pv(c, s_ref, k0, size):
        s = s_ref[c, :, :size]
        m_old = m_s[c]
        m_new = jnp.maximum(m_old, jnp.max(s, axis=1, keepdims=True))
        alpha = jnp.exp2(m_old - m_new)
        p = jnp.exp2(s - m_new).astype(BF16)
        acc_s[c] = alpha * acc_s[c] + jnp.dot(p, v1_ref[pl.ds(k0, size), :],
                                              preferred_element_type=F32)
        m_s[c] = m_new

    chains = range(ATT_CHAINS)

    @pl.when(i < n_ctx_qsteps)
    def _():
        for c in chains:
            scores(c, s_c, 0, n_ctx)
        for c in chains:
            softmax_pv(c, s_c, 0, n_ctx)

    tiles = [(s_c, 0, n_ctx)] + [((s_a, s_b)[t % 2], n_ctx + t * TK, TK) for t in range(n_lat // TK)]

    @pl.when(i >= n_ctx_qsteps)
    def _():
        for c in chains:
            scores(c, *tiles[0])
        for t, tile in enumerate(tiles):
            if t + 1 < len(tiles):
                for c in chains:
                    scores(c, *tiles[t + 1])
            for c in chains:
                softmax_pv(c, *tile)

    for c in chains:
        acc = acc_s[c]
        o = (acc[:, :KV_LORA] / acc[:, KV_LORA:KV_LORA + 1]).astype(BF16)
        for h in range(HEADS):
            o_ref[c * TQ:(c + 1) * TQ, h * KV_LORA:(h + 1) * KV_LORA] = o[h * TQ:(h + 1) * TQ, :]


def _attention(q, kv, v1, n_ctx):
    b, nt, _ = kv.shape
    rows = TQ * HEADS
    step = ATT_CHAINS * TQ
    n_lat = nt - n_ctx
    assert n_lat % TK == 0 and n_ctx % step == 0 and n_lat % step == 0
    kern = functools.partial(_attn_kernel, n_ctx=n_ctx, n_lat=n_lat, n_ctx_qsteps=n_ctx // step)
    return pl.pallas_call(
        kern,
        out_shape=jax.ShapeDtypeStruct((b, nt, HEADS * KV_LORA), BF16),
        grid=(b, nt // step),
        in_specs=[pl.BlockSpec((None, step, HEADS * QH), lambda bi, i: (bi, i, 0)),
                  pl.BlockSpec((None, nt, QH), lambda bi, i: (bi, 0, 0)),
                  pl.BlockSpec((None, nt, 2 * LANES), lambda bi, i: (bi, 0, 0))],
        out_specs=pl.BlockSpec((None, step, HEADS * KV_LORA), lambda bi, i: (bi, i, 0)),
        scratch_shapes=[pltpu.VMEM((ATT_CHAINS, rows, QH), BF16),
                        pltpu.VMEM((ATT_CHAINS, rows, 1), F32),
                        pltpu.VMEM((ATT_CHAINS, rows, 2 * LANES), F32),
                        pltpu.VMEM((ATT_CHAINS, rows, TK), F32),
                        pltpu.VMEM((ATT_CHAINS, rows, TK), F32),
                        pltpu.VMEM((ATT_CHAINS, rows, n_ctx), F32)],
        compiler_params=_cparams(("parallel", "arbitrary"), VMEM_LIMIT),
        name="mla_attention",
    )(q, kv, v1)


def _lru_tile(j, nct, ntl, reverse):
    if not reverse:
        return j
    return jnp.where(j < nct, nct - 1 - j, ntl - 1 - (j - nct))


def _lru_kernel(x_ref, prev_ref, next_ref, cw_ref, cb_ref, w_ref, ba_ref, bx_ref, lam_ref,
                o_ref, a_s, u_s, h_s, *, nct, ntl, reverse):
    j = pl.program_id(1)
    tile = _lru_tile(j, nct, ntl, reverse)

    @pl.when(j == 0)
    def _():
        h_s[...] = jnp.zeros(h_s.shape, F32)

    x = x_ref[...].astype(F32)
    has_prev = jnp.logical_and(tile != 0, tile != nct)
    has_next = jnp.logical_and(tile != nct - 1, tile != ntl - 1)
    prev = jnp.where(has_prev, prev_ref[...].astype(F32), 0.0)
    nxt = jnp.where(has_next, next_ref[...].astype(F32), 0.0)
    row = lax.broadcasted_iota(jnp.int32, (TM, D), 0)
    xm1 = jnp.where(row == 0, prev[HALO - 1:HALO, :], pltpu.roll(x, 1, 0))
    xm2 = jnp.where(row == 0, prev[HALO - 2:HALO - 1, :],
                    jnp.where(row == 1, prev[HALO - 1:HALO, :], pltpu.roll(x, 2, 0)))
    xp1 = jnp.where(row == TM - 1, nxt[0:1, :], pltpu.roll(x, TM - 1, 0))
    cw = cw_ref[...]
    xl = cb_ref[...] + xm2 * cw[0:1, :] + xm1 * cw[1:2, :] + x * cw[2:3, :] + xp1 * cw[3:4, :]

    z = -lam_ref[...]
    softplus = jnp.maximum(z, 0.0) + jnp.log1p(jnp.exp(-jnp.abs(z)))
    rate = -LRU_C * softplus
    for g in range(D // LANES):
        sl = slice(g * LANES, (g + 1) * LANES)
        xg = xl[:, sl]
        pre = jnp.dot(xg.astype(BF16), w_ref[g], preferred_element_type=F32)
        r = _sigmoid(pre[:, :LANES] + ba_ref[:, sl])
        ig = _sigmoid(pre[:, LANES:] + bx_ref[:, sl])
        log_a = rate[:, sl] * r
        a = jnp.exp(log_a)
        a_s[:, sl] = a
        u_s[:, sl] = jnp.sqrt(-jnp.tanh(log_a) * (1.0 + a * a)) * (ig * xg)

    srow = lax.broadcasted_iota(jnp.int32, (SUBLANES, D), 0)
    n_chunks = TM // SUBLANES

    def chunk(c, h_prev):
        cc = (n_chunks - 1 - c) if reverse else c
        r0 = pl.multiple_of(cc * SUBLANES, SUBLANES)
        a = a_s[pl.ds(r0, SUBLANES), :]
        u = u_s[pl.ds(r0, SUBLANES), :]
        for s in (1, 2, 4):
            if reverse:
                keep = srow < SUBLANES - s
                shift = SUBLANES - s
            else:
                keep = srow >= s
                shift = s
            u = u + a * jnp.where(keep, pltpu.roll(u, shift, 0), 0.0)
            a = a * jnp.where(keep, pltpu.roll(a, shift, 0), 1.0)
        h = u + a * h_prev
        o_ref[pl.ds(r0, SUBLANES), :] = h
        edge = h[0:1, :] if reverse else h[SUBLANES - 1:SUBLANES, :]
        return jnp.broadcast_to(edge, (SUBLANES, D))

    h_s[...] = lax.fori_loop(0, n_chunks, chunk, h_s[...])


def _lru(zb, cw, cb, wd, ba, bx, lam, nct, reverse):
    b, nt, _ = zb.shape
    ntl = nt // TM
    hb = TM // HALO
    n_hb = nt // HALO
    tile = lambda j: _lru_tile(j, nct, ntl, reverse)
    kern = functools.partial(_lru_kernel, nct=nct, ntl=ntl, reverse=reverse)
    vec = lambda n: pl.BlockSpec((n, D), lambda bi, j: (0, 0))
    return pl.pallas_call(
        kern,
        out_shape=jax.ShapeDtypeStruct((b, nt, D), F32),
        grid=(b, ntl),
        in_specs=[pl.BlockSpec((None, TM, D), lambda bi, j: (bi, tile(j), 0)),
                  pl.BlockSpec((None, HALO, D),
                               lambda bi, j: (bi, jnp.maximum(tile(j) * hb - 1, 0), 0)),
                  pl.BlockSpec((None, HALO, D),
                               lambda bi, j: (bi, jnp.minimum((tile(j) + 1) * hb, n_hb - 1), 0)),
                  vec(CONV_W), vec(1),
                  pl.BlockSpec((D // LANES, LANES, 2 * LANES), lambda bi, j: (0, 0, 0)),
                  vec(1), vec(1), vec(1)],
        out_specs=pl.BlockSpec((None, TM, D), lambda bi, j: (bi, tile(j), 0)),
        scratch_shapes=[pltpu.VMEM((TM, D), F32), pltpu.VMEM((TM, D), F32),
                        pltpu.VMEM((SUBLANES, D), F32)],
        compiler_params=_cparams(("arbitrary", "arbitrary"), VMEM_LIMIT),
        name="rglru_bwd" if reverse else "rglru_fwd",
    )(zb, zb, zb, cw, cb, wd, ba, bx, lam)


def _merge_kernel(o_ref, hf_ref, hb_ref, gr_ref, ga_ref, gl_ref, x_ref, mod_ref, bg_ref, nf_ref,
                  wvo_ref, wol_ref, wout_ref, wr_ref, br_ref,
                  xn_ref, h2_ref, rg_ref, rt_ref, meta_ref, cnt_ref, carry_s):
    first = jnp.logical_and(pl.program_id(0) == 0, pl.program_id(1) == 0)

    @pl.when(first)
    def _():
        carry_s[...] = jnp.zeros(carry_s.shape, F32)

    wr = wr_ref[...]
    w_hi = wr.astype(BF16)
    w_lo = (wr - w_hi.astype(F32)).astype(BF16)

    gr = gr_ref[...].astype(F32)
    gelu = 0.5 * gr * (1.0 + jnp.tanh(math.sqrt(2.0 / math.pi) * (gr + 0.044715 * (gr * gr * gr))))
    lru = ((hf_ref[...] + hb_ref[...]) * gelu).astype(BF16)
    y_att = jnp.dot(o_ref[...], wvo_ref[...], preferred_element_type=F32)
    y_lru = jnp.dot(lru, wol_ref[...], preferred_element_type=F32)
    m = (_sigmoid(ga_ref[...].astype(F32) + bg_ref[0]) * y_att
         + _sigmoid(gl_ref[...].astype(F32) + bg_ref[1]) * y_lru)
    out = jnp.dot(m.astype(BF16), wout_ref[...], preferred_element_type=F32)
    xn = x_ref[...] + mod_ref[2] * out
    xn_ref[...] = xn
    h2 = _rms(xn, nf_ref[...]) * (1.0 + mod_ref[4]) + mod_ref[3]
    h_hi = h2.astype(BF16)
    h2_ref[...] = h_hi
    h_lo = (h2 - h_hi.astype(F32)).astype(BF16)
    logits = (jnp.dot(h_hi, w_hi, preferred_element_type=F32)
              + (jnp.dot(h_lo, w_hi, preferred_element_type=F32)
                 + jnp.dot(h_hi, w_lo, preferred_element_type=F32))) + br_ref[...]
    lane = lax.broadcasted_iota(jnp.int32, (TM, LANES), 1)
    lane_f = lane.astype(F32)
    l = logits
    vals, idxs = [], []
    for _ in range(TOP_K):
        mx = jnp.max(l, axis=1, keepdims=True)
        ix = jnp.min(jnp.where(l == mx, lane_f, float(LANES)), axis=1, keepdims=True)
        vals.append(mx)
        idxs.append(ix)
        l = jnp.where(lane_f == ix, NEG_INF, l)
    chosen = jnp.where(jnp.logical_and(l == NEG_INF, lane < N_EXPERTS), 1.0, 0.0)
    es = [jnp.exp(v - vals[0]) for v in vals]
    den = es[0] + es[1] + es[2] + es[3]

    r_i = lax.broadcasted_iota(jnp.int32, (TM, TM), 0)
    c_i = lax.broadcasted_iota(jnp.int32, (TM, TM), 1)
    tri = jnp.where(r_i > c_i, 1.0, 0.0).astype(BF16)
    before = jnp.dot(tri, chosen.astype(BF16), preferred_element_type=F32)
    cnt_t = jnp.sum(chosen, axis=0, keepdims=True)
    seg_len = jnp.floor((cnt_t + (MOE_CH - 1.0)) * (1.0 / MOE_CH)) * MOE_CH
    e_r = lax.broadcasted_iota(jnp.int32, (LANES, LANES), 0)
    e_c = lax.broadcasted_iota(jnp.int32, (LANES, LANES), 1)
    upper = jnp.where(e_r < e_c, 1.0, 0.0).astype(BF16)
    seg_off = jnp.dot(jnp.broadcast_to(seg_len, (SUBLANES, LANES)).astype(BF16), upper,
                      preferred_element_type=F32)[0:1, :]
    slot_of = before + seg_off
    rg = jnp.zeros((TM, LANES), F32)
    for k in range(TOP_K):
        slot = jnp.sum(jnp.where(lane_f == idxs[k], slot_of, 0.0), axis=1, keepdims=True)
        rg = jnp.where(lane == k, es[k] / den, rg)
        rg = jnp.where(lane == TOP_K + k, slot, rg)
    rg_ref[...] = rg
    rt_ref[...] = jnp.transpose(rg)[0:SUBLANES, :]
    row8 = lax.broadcasted_iota(jnp.int32, (SUBLANES, LANES), 0)
    carry = carry_s[...]
    meta_ref[...] = jnp.where(row8 == 0, cnt_t, jnp.where(row8 == 1, carry, 0.0)).astype(jnp.int32)
    carry = carry + jnp.floor((cnt_t + (SUBLANES - 1.0)) * (1.0 / SUBLANES)) * SUBLANES
    carry_s[...] = carry
    cnt_ref[...] = jnp.broadcast_to(carry, cnt_ref.shape).astype(jnp.int32)


def _merge(o, hf, hb, zb, x, mods, bg, nf, wvo, wol, wout, wr, br, nct, tile0):
    b, nt, _ = x.shape
    ntl = nt // TM
    tok = lambda c: pl.BlockSpec((None, TM, D), lambda bi, j: (bi, j + tile0, c))
    full = lambda *s: pl.BlockSpec(s, lambda bi, j: (0,) * len(s))
    out_nt = nt - tile0 * TM
    return pl.pallas_call(
        _merge_kernel,
        out_shape=(jax.ShapeDtypeStruct((b, out_nt, D), F32),
                   jax.ShapeDtypeStruct((b, out_nt, D), BF16),
                   jax.ShapeDtypeStruct((b, out_nt, LANES), F32),
                   jax.ShapeDtypeStruct((b, out_nt // TM, SUBLANES, TM), F32),
                   jax.ShapeDtypeStruct((b, out_nt // TM, SUBLANES, LANES), jnp.int32),
                   jax.ShapeDtypeStruct((SUBLANES, LANES), jnp.int32)),
        grid=(b, ntl - tile0),
        in_specs=[tok(0), tok(0), tok(0), tok(1), tok(2), tok(3), tok(0),
                  pl.BlockSpec((None, None, 6, 1, D),
                               lambda bi, j: (bi, _seg(j + tile0, nct), 0, 0, 0)),
                  full(2, 1, D), full(1, D), full(D, D), full(D, D), full(D, D),
                  full(D, LANES), full(1, LANES)],
        out_specs=(pl.BlockSpec((None, TM, D), lambda bi, j: (bi, j, 0)),
                   pl.BlockSpec((None, TM, D), lambda bi, j: (bi, j, 0)),
                   pl.BlockSpec((None, TM, LANES), lambda bi, j: (bi, j, 0)),
                   pl.BlockSpec((None, None, SUBLANES, TM), lambda bi, j: (bi, j, 0, 0)),
                   pl.BlockSpec((None, None, SUBLANES, LANES), lambda bi, j: (bi, j, 0, 0)),
                   pl.BlockSpec((SUBLANES, LANES), lambda bi, j: (0, 0))),
        scratch_shapes=[pltpu.VMEM((1, LANES), F32)],
        compiler_params=_cparams(("arbitrary", "arbitrary"), VMEM_LIMIT),
        name="merge_route",
    )(o, hf, hb, zb, zb, zb, x, mods, bg, nf, wvo, wol, wout, wr, br)


def _dispatch_kernel(meta_ref, h2_ref, rt_ref, xs_ref, sorted_s, zero_s, sem, *, n_blk, ntl):
    bi = pl.program_id(0)
    j = pl.program_id(1)
    first = jnp.logical_and(bi == 0, j == 0)

    @pl.when(first)
    def _():
        zero_s[...] = jnp.zeros(zero_s.shape, F32)

        def pad_copy(i):
            r0 = pl.multiple_of(i * SUBLANES, SUBLANES)
            return pltpu.make_async_copy(zero_s.at[pl.ds(0, SUBLANES), :],
                                         xs_ref.at[pl.ds(r0, SUBLANES), :], sem)

        def blk_copy(i):
            r0 = pl.multiple_of(i * MOE_R, MOE_R)
            return pltpu.make_async_copy(zero_s, xs_ref.at[pl.ds(r0, MOE_R), :], sem)

        def blk_issue(i, c):
            blk_copy(i).start()
            return c

        def blk_drain(i, c):
            blk_copy(i).wait()
            return c
        n_used = meta_ref[META_NUSED]
        lax.fori_loop(n_used, n_blk, blk_issue, 0)
        lax.fori_loop(n_used, n_blk, blk_drain, 0)

        def per_expert(e, c):
            lo = meta_ref[META_TAIL + e] // SUBLANES
            hi = meta_ref[META_END + e] // SUBLANES

            def issue(r, c2):
                pad_copy(r).start()
                return c2

            def drain(r, c2):
                pad_copy(r).wait()
                return c2
            lax.fori_loop(lo, hi, issue, 0)
            lax.fori_loop(lo, hi, drain, 0)
            return c
        lax.fori_loop(0, N_EXPERTS, per_expert, 0)

    slot_i = lax.broadcasted_iota(jnp.int32, (MOE_SLOTS, TM), 0).astype(F32)
    hit = slot_i == rt_ref[TOP_K:TOP_K + 1, :]
    for k in range(1, TOP_K):
        hit = jnp.logical_or(hit, slot_i == rt_ref[TOP_K + k:TOP_K + k + 1, :])
    onehot = jnp.where(hit, 1.0, 0.0).astype(BF16)
    tile = bi * ntl + j
    cur = tile % 2
    sorted_s[cur] = jnp.dot(onehot, h2_ref[...], preferred_element_type=F32)

    def refs_of(buf):
        return lambda slot, row: (sorted_s.at[buf, pl.ds(slot, MOE_CH), :],
                                  xs_ref.at[pl.ds(row, MOE_CH), :])

    @pl.when(tile > 0)
    def _():
        _moe_chunk_copies(meta_ref, tile - 1, sem, refs_of(1 - cur), start=False)
    _moe_chunk_copies(meta_ref, tile, sem, refs_of(cur), start=True)

    @pl.when(tile == pl.num_programs(0) * ntl - 1)
    def _():
        _moe_chunk_copies(meta_ref, tile, sem, refs_of(cur), start=False)


def _moe_chunk_copies(meta_ref, tile, sem, refs, *, start):
    base = META_TILES + tile * META_PER_TILE

    def piece(c, carry):
        src, dst = refs(pl.multiple_of(meta_ref[base + 1 + 2 * c], MOE_CH),
                        pl.multiple_of(meta_ref[base + 2 + 2 * c], SUBLANES))
        cp = pltpu.make_async_copy(src, dst, sem)
        if start:
            cp.start()
        else:
            cp.wait()
        return carry
    lax.fori_loop(0, meta_ref[base], piece, 0)


def _dispatch(meta, h2, rt, n_rows):
    b, nt, _ = h2.shape
    ntl = nt // TM
    kern = functools.partial(_dispatch_kernel, n_blk=n_rows // MOE_R, ntl=ntl)
    return pl.pallas_call(
        kern,
        out_shape=jax.ShapeDtypeStruct((n_rows, D), F32),
        grid_spec=pltpu.PrefetchScalarGridSpec(
            num_scalar_prefetch=1,
            grid=(b, ntl),
            in_specs=[pl.BlockSpec((None, TM, D), lambda bi, j, meta: (bi, j, 0)),
                      pl.BlockSpec((None, None, SUBLANES, TM), lambda bi, j, meta: (bi, j, 0, 0))],
            out_specs=pl.BlockSpec(memory_space=pl.ANY),
            scratch_shapes=[pltpu.VMEM((2, MOE_SLOTS, D), F32),
                            pltpu.VMEM((MOE_R, D), F32),
                            pltpu.SemaphoreType.DMA]),
        compiler_params=_cparams(("arbitrary", "arbitrary"), VMEM_LIMIT),
        name="moe_dispatch",
    )(meta, h2, rt)


def _expert_kernel(be_ref, nu_ref, x_ref, wg_ref, bg_ref, wu_ref, bu_ref, wd_ref, bd_ref,
                   y_ref, wg_s, wu_s, wd_s):
    blk = pl.program_id(0)
    used = blk < nu_ref[0]
    prev = be_ref[jnp.maximum(blk - 1, 0)]
    fresh = jnp.logical_or(blk == 0, be_ref[blk] != prev)

    @pl.when(jnp.logical_and(used, fresh))
    def _():
        wg_s[...] = wg_ref[...].astype(BF16)
        wu_s[...] = wu_ref[...].astype(BF16)
        wd_s[...] = wd_ref[...].astype(BF16)

    @pl.when(used)
    def _():
        x = x_ref[...].astype(BF16)
        g = jnp.dot(x, wg_s[...], preferred_element_type=F32) + bg_ref[...]
        u = jnp.dot(x, wu_s[...], preferred_element_type=F32) + bu_ref[...]
        g = jnp.minimum(g, SWIGLU_LIMIT)
        u = jnp.clip(u, -SWIGLU_LIMIT, SWIGLU_LIMIT)
        act = g * _sigmoid(SWIGLU_ALPHA * g) * (u + 1.0)
        y_ref[...] = jnp.dot(act.astype(BF16), wd_s[...], preferred_element_type=F32) + bd_ref[...]

    @pl.when(jnp.logical_not(used))
    def _():
        y_ref[...] = jnp.zeros(y_ref.shape, F32)


def _experts(blk_expert, n_used, xs, wg, bg, wu, bu, wd, bd, layer):
    n_rows = xs.shape[0]
    n_blk = n_rows // MOE_R
    dff = wg.shape[-1]
    e_of = lambda i, be, nu: be[jnp.minimum(i, nu[0] - 1)]
    wspec = lambda r, c: pl.BlockSpec((None, None, r, c), lambda i, be, nu: (layer, e_of(i, be, nu), 0, 0))
    return pl.pallas_call(
        _expert_kernel,
        out_shape=jax.ShapeDtypeStruct((n_rows, D), F32),
        grid_spec=pltpu.PrefetchScalarGridSpec(
            num_scalar_prefetch=2,
            grid=(n_blk,),
            in_specs=[pl.BlockSpec((MOE_R, D), lambda i, be, nu: (jnp.minimum(i, nu[0] - 1), 0)),
                      wspec(D, dff), wspec(1, dff), wspec(D, dff), wspec(1, dff),
                      wspec(dff, D), wspec(1, D)],
            out_specs=pl.BlockSpec((MOE_R, D), lambda i, be, nu: (i, 0)),
            scratch_shapes=[pltpu.VMEM((D, dff), BF16), pltpu.VMEM((D, dff), BF16),
                            pltpu.VMEM((dff, D), BF16)]),
        compiler_params=_cparams(("arbitrary",), VMEM_LIMIT),
        name="moe_experts",
    )(blk_expert, n_used, xs, wg, bg, wu, bu, wd, bd)


def _combine_kernel(meta_ref, x_ref, rg_ref, mod_ref, fn_ref, ys_ref, o_ref, buf_s, sem, *, final, ntl):
    bi = pl.program_id(0)
    j = pl.program_id(1)

    tile = bi * ntl + j
    cur = tile % 2

    def refs_of(buf):
        return lambda slot, row: (ys_ref.at[pl.ds(row, MOE_CH), :],
                                  buf_s.at[buf, pl.ds(slot, MOE_CH), :])

    @pl.when(tile == 0)
    def _():
        buf_s[...] = jnp.zeros(buf_s.shape, F32)
        _moe_chunk_copies(meta_ref, tile, sem, refs_of(cur), start=True)

    _moe_chunk_copies(meta_ref, tile, sem, refs_of(cur), start=False)

    @pl.when(tile < pl.num_programs(0) * ntl - 1)
    def _():
        _moe_chunk_copies(meta_ref, tile + 1, sem, refs_of(1 - cur), start=True)

    rg = rg_ref[...]
    slot_i = lax.broadcasted_iota(jnp.int32, (TM, MOE_SLOTS), 1).astype(F32)
    gate = jnp.zeros((TM, MOE_SLOTS), F32)
    for k in range(TOP_K):
        gate = jnp.where(slot_i == rg[:, TOP_K + k:TOP_K + k + 1], rg[:, k:k + 1], gate)
    y = jnp.dot(gate.astype(BF16), buf_s[cur].astype(BF16), preferred_element_type=F32)
    xo = x_ref[...] + mod_ref[5] * y
    if final:
        xo = _rms(xo, fn_ref[...])
    o_ref[...] = xo


def _combine(meta, x, rg, mods, fn, ys, nct, final):
    b, nt, _ = x.shape
    ntl = nt // TM
    kern = functools.partial(_combine_kernel, final=final, ntl=ntl)
    return pl.pallas_call(
        kern,
        out_shape=jax.ShapeDtypeStruct((b, nt, D), F32),
        grid_spec=pltpu.PrefetchScalarGridSpec(
            num_scalar_prefetch=1,
            grid=(b, ntl),
            in_specs=[pl.BlockSpec((None, TM, D), lambda bi, j, meta: (bi, j, 0)),
                      pl.BlockSpec((None, TM, LANES), lambda bi, j, meta: (bi, j, 0)),
                      pl.BlockSpec((None, None, 6, 1, D),
                                   lambda bi, j, meta: (bi, _seg(j, nct), 0, 0, 0)),
                      pl.BlockSpec((1, D), lambda bi, j, meta: (0, 0)),
                      pl.BlockSpec(memory_space=pl.ANY)],
            out_specs=pl.BlockSpec((None, TM, D), lambda bi, j, meta: (bi, j, 0)),
            scratch_shapes=[pltpu.VMEM((2, MOE_SLOTS, D), F32), pltpu.SemaphoreType.DMA]),
        compiler_params=_cparams(("arbitrary", "arbitrary"), VMEM_LIMIT),
        name="moe_combine",
    )(meta, x, rg, mods, fn, ys)


_ROT_PARTNER = tuple(list(range(16, 32)) + list(range(0, 16)) + list(range(48, 64)) + list(range(32, 48)))


def _rope_tables(n_ctx, n_lat):
    n = jnp.arange(n_lat)
    row = (n // GRID_W).astype(F32)
    col = (n % GRID_W).astype(F32)
    n_freq = QK_ROPE // 4
    inv = ROPE_THETA ** (-jnp.arange(n_freq, dtype=F32) / n_freq)
    ar = row[:, None] * inv
    ac = col[:, None] * inv
    c64 = jnp.concatenate([jnp.cos(ar), jnp.cos(ar), jnp.cos(ac), jnp.cos(ac)], axis=1)
    s64 = jnp.concatenate([-jnp.sin(ar), jnp.sin(ar), -jnp.sin(ac), jnp.sin(ac)], axis=1)
    c64 = jnp.concatenate([jnp.ones((n_ctx, QK_ROPE), F32), c64], axis=0)
    s64 = jnp.concatenate([jnp.zeros((n_ctx, QK_ROPE), F32), s64], axis=0)
    nt = n_ctx + n_lat
    qs = ATTN_SCALE * LOG2E
    cq = qs * jnp.concatenate([jnp.ones((nt, QK_NOPE), F32), c64, c64], axis=1)
    sq = qs * jnp.concatenate([jnp.zeros((nt, QK_NOPE), F32), s64, s64], axis=1)
    ck = jnp.concatenate([c64, s64], axis=1)
    return cq, sq, ck


def _block_diag_pairs(w):
    w = w.reshape(LRU_BLOCKS // 2, 2, LRU_BW, LRU_BW)
    z = jnp.zeros((LRU_BLOCKS // 2, LRU_BW, LRU_BW), w.dtype)
    top = jnp.concatenate([w[:, 0], z], axis=2)
    bot = jnp.concatenate([z, w[:, 1]], axis=2)
    return jnp.concatenate([top, bot], axis=1)


def kernel(x, c, ctx, c_ctx, w_ada, b_ada, norm_mix, norm_ffn, w_in, b_branch_gate, q_norm, w_uq, kv_norm, w_ukv, w_o_attn, conv_w, conv_b, lru_w_a, lru_b_a, lru_w_x, lru_b_x, lru_lambda, w_o_lru, w_out, w_router, b_router, w_exp_gate, b_exp_gate, w_exp_up, b_exp_up, w_exp_down, b_exp_down, final_norm):
    depth = w_in.shape[0]
    b, n_lat, _ = x.shape
    n_ctx = ctx.shape[1]
    assert n_ctx % TM == 0 and n_lat % TK == 0 and b + 1 <= SUBLANES
    nct = n_ctx // TM
    nt = n_ctx + n_lat
    ntl = nt // TM
    perm = jnp.array(_ROT_PARTNER)

    xs = jnp.concatenate([ctx, x], axis=1)
    cond8 = jnp.zeros((SUBLANES, D), F32).at[:b].set(c).at[b].set(c_ctx)
    cq_t, sq_t, ck_t = _rope_tables(n_ctx, n_lat)

    for l in range(depth):
        last = l == depth - 1
        tile0 = nct if last else 0

        m8 = _ada(cond8, w_ada[l], b_ada[l][None, :]).reshape(SUBLANES, 6, 1, D)
        mods = jnp.stack([jnp.broadcast_to(m8[b], (b,) + m8.shape[1:]), m8[:b]], axis=1)

        wi = w_in[l]
        kpe_w = wi[:, Q_LORA + KV_LORA:Q_LORA + KV_LORA + QK_ROPE]
        wa = jnp.concatenate([wi[:, :Q_LORA + KV_LORA + QK_ROPE], kpe_w[:, perm]], axis=1).astype(BF16)
        wb = wi[:, Q_LORA + KV_LORA + QK_ROPE:].astype(BF16)
        uq = w_uq[l].reshape(Q_LORA, HEADS, QK_DIM).transpose(1, 0, 2)
        ukv = w_ukv[l].reshape(KV_LORA, HEADS, QK_NOPE + V_HEAD).transpose(1, 0, 2)
        wqa, wvo = _fold(uq[:, :, :QK_NOPE], ukv[:, :, :QK_NOPE], ukv[:, :, QK_NOPE:],
                         w_o_attn[l].reshape(HEADS, V_HEAD, D))
        pe = uq[:, :, QK_NOPE:]
        pe_sw = pe[:, :, perm]
        zq = jnp.zeros((HEADS, Q_LORA, QK_NOPE), F32)
        wq1 = jnp.concatenate([wqa, pe, pe], axis=2)
        wq2 = jnp.concatenate([zq, pe_sw, pe_sw], axis=2)
        wq = jnp.concatenate([wq1.transpose(1, 0, 2).reshape(Q_LORA, HEADS * QH),
                              wq2.transpose(1, 0, 2).reshape(Q_LORA, HEADS * QH)], axis=1).astype(BF16)
        wvo = wvo.reshape(D, D).astype(BF16)
        wr = jnp.zeros((D, LANES), F32).at[:, :N_EXPERTS].set(w_router[l])
        br = jnp.full((1, LANES), NEG_INF, F32).at[0, :N_EXPERTS].set(b_router[l])

        za, zb = _inproj(xs, mods, norm_mix[l][None, :], wa, wb, nct)
        q, kv, v1 = _mlaproj(za, q_norm[l][None, :], kv_norm[l][None, :], wq, cq_t, sq_t, ck_t)
        o = _attention(q, kv, v1, n_ctx)
        hs = []
        for d in range(2):
            wd = jnp.concatenate([_block_diag_pairs(lru_w_a[l, d]), _block_diag_pairs(lru_w_x[l, d])],
                                 axis=2).astype(BF16)
            hs.append(_lru(zb, conv_w[l], conv_b[l][None, :], wd, lru_b_a[l, d][None, :],
                           lru_b_x[l, d][None, :], lru_lambda[l, d][None, :], nct, reverse=d == 1))
        xn, h2, rg, rt, tile_meta, cnt = _merge(
            o, hs[0], hs[1], zb, xs, mods, b_branch_gate[l].reshape(2, 1, D), norm_ffn[l][None, :],
            wvo, w_o_lru[l].astype(BF16), w_out[l].astype(BF16), wr, br, nct, tile0)

        counts = cnt[0, :N_EXPERTS]
        padded = ((counts + (MOE_CH - 1) + MOE_R - 1) // MOE_R) * MOE_R
        pad_end = jnp.cumsum(padded)
        pad_start = pad_end - padded
        moe_nt = xn.shape[1]
        n_tiles = b * moe_nt // TM
        max_rows = (b * moe_nt * TOP_K + n_tiles * N_EXPERTS * (SUBLANES - 1)
                    + N_EXPERTS * (MOE_CH - 1))
        n_blk = max_rows // MOE_R + N_EXPERTS
        n_rows = n_blk * MOE_R
        blk_start = jnp.arange(n_blk, dtype=jnp.int32) * MOE_R
        blk_expert = jnp.minimum(jnp.sum(blk_start[:, None] >= pad_end[None, :], axis=1),
                                 N_EXPERTS - 1).astype(jnp.int32)
        n_used = (pad_end[-1:] // MOE_R).astype(jnp.int32)
        cnt_t = tile_meta[:, :, 0, :N_EXPERTS].reshape(n_tiles, N_EXPERTS)
        row0_t = pad_start[None, :] + tile_meta[:, :, 1, :N_EXPERTS].reshape(n_tiles, N_EXPERTS)
        n_ch = (cnt_t + MOE_CH - 1) // MOE_CH
        ch_end = jnp.cumsum(n_ch, axis=1)
        ch_start = ch_end - n_ch
        c_ids = jnp.arange(MOE_MAX_COPIES, dtype=jnp.int32)
        e_of_c = jnp.minimum(jnp.sum(c_ids[None, :, None] >= ch_end[:, None, :], axis=-1), N_EXPERTS - 1)
        is_e = e_of_c[:, :, None] == jnp.arange(N_EXPERTS, dtype=jnp.int32)
        pick = lambda v: jnp.sum(jnp.where(is_e, v[:, None, :], 0), axis=-1)
        first_c = pick(ch_start)
        piece = (c_ids[None, :] - first_c) * MOE_CH
        copies = jnp.stack([first_c * MOE_CH + piece, pick(row0_t) + piece], axis=-1)
        tile_tab = jnp.concatenate(
            [ch_end[:, -1:], copies.reshape(n_tiles, 2 * MOE_MAX_COPIES),
             jnp.zeros((n_tiles, META_PER_TILE - 1 - 2 * MOE_MAX_COPIES), jnp.int32)], axis=1)
        meta = jnp.concatenate([pad_start, pad_start + counts, pad_end,
                                jnp.broadcast_to(n_used, (N_EXPERTS,)),
                                tile_tab.reshape(-1)]).astype(jnp.int32)

        xsort = _dispatch(meta, h2, rt, n_rows)
        ys = _experts(blk_expert, n_used, xsort, w_exp_gate, b_exp_gate[:, :, None, :],
                      w_exp_up, b_exp_up[:, :, None, :], w_exp_down, b_exp_down[:, :, None, :], l)
        xs = _combine(meta, xn, rg, mods, final_norm[None, :], ys, nct - tile0, last)
    return xs
```

```python
import functools
import math

import jax
import jax.numpy as jnp
from jax import lax
from jax.experimental import pallas as pl
from jax.experimental.pallas import tpu as pltpu

D = 1024
GRID_W = 64
HEADS = 8
QK_NOPE = 128
QK_ROPE = 64
QK_DIM = QK_NOPE + QK_ROPE
V_HEAD = 128
Q_LORA = 256
KV_LORA = 128
ROPE_THETA = 10000.0
ATTN_SCALE = QK_DIM ** -0.5
LRU_BLOCKS = 16
LRU_BW = D // LRU_BLOCKS
CONV_W = 4
LRU_C = 8.0
N_EXPERTS = 32
TOP_K = 4
SWIGLU_ALPHA = 1.702
SWIGLU_LIMIT = 7.0
NORM_EPS = 1e-6
D_ZA = Q_LORA + KV_LORA + 2 * QK_ROPE
D_ZB = 4 * D
QH = 256

LANES = 128
SUBLANES = 8
TM = 256
HALO = 16
TQ = 128
ATT_CHAINS = 1
TK = 2048
MOE_R = 512
MOE_CH = 16
MOE_SLOTS = TM * TOP_K + N_EXPERTS * MOE_CH
META_TAIL = N_EXPERTS
META_END = 2 * N_EXPERTS
META_NUSED = 3 * N_EXPERTS
META_TILES = 4 * N_EXPERTS
MOE_MAX_COPIES = MOE_SLOTS // MOE_CH
META_PER_TILE = 256
assert 1 + 2 * MOE_MAX_COPIES <= META_PER_TILE
VMEM_LIMIT = 56 * 1024 * 1024

F32 = jnp.float32
BF16 = jnp.bfloat16
NEG_INF = float("-inf")
LOG2E = 1.4426950408889634


def _cparams(sem, vmem=None):
    return pltpu.CompilerParams(dimension_semantics=sem, vmem_limit_bytes=vmem)


def _rms(x, g):
    return x * lax.rsqrt(jnp.mean(x * x, axis=-1, keepdims=True) + NORM_EPS) * g


def _sigmoid(x):
    return 0.5 * jnp.tanh(0.5 * x) + 0.5


def _seg(i, nct):
    return jnp.where(i < nct, 0, 1)


def _ada_kernel(c_ref, w_ref, b_ref, o_ref):
    c = c_ref[...]
    s = (c * _sigmoid(c)).astype(BF16)
    o_ref[...] = jnp.dot(s, w_ref[...].astype(BF16), preferred_element_type=F32) + b_ref[...]


def _ada(cond8, w, b, tn=768):
    n = w.shape[1]
    return pl.pallas_call(
        _ada_kernel,
        out_shape=jax.ShapeDtypeStruct((SUBLANES, n), F32),
        grid=(n // tn,),
        in_specs=[pl.BlockSpec((SUBLANES, D), lambda j: (0, 0)),
                  pl.BlockSpec((D, tn), lambda j: (0, j)),
                  pl.BlockSpec((1, tn), lambda j: (0, j))],
        out_specs=pl.BlockSpec((SUBLANES, tn), lambda j: (0, j)),
        compiler_params=_cparams(("parallel",)),
        name="adaln",
    )(cond8, w, b)


def _fold_kernel(nope_ref, wuk_ref, wuv_ref, wo_ref, wqa_ref, wvo_ref):
    hp = lax.Precision.HIGHEST
    wqa_ref[...] = lax.dot_general(nope_ref[...], wuk_ref[...], (((1,), (1,)), ((), ())),
                                   precision=hp, preferred_element_type=F32)
    wvo_ref[...] = jnp.dot(wuv_ref[...], wo_ref[...], precision=hp, preferred_element_type=F32)


def _fold(nope, wuk, wuv, wo):
    sq = lambda *s: pl.BlockSpec((None,) + s, lambda h: (h, 0, 0))
    return pl.pallas_call(
        _fold_kernel,
        out_shape=(jax.ShapeDtypeStruct((HEADS, Q_LORA, KV_LORA), F32),
                   jax.ShapeDtypeStruct((HEADS, KV_LORA, D), F32)),
        grid=(HEADS,),
        in_specs=[sq(Q_LORA, QK_NOPE), sq(KV_LORA, QK_NOPE), sq(KV_LORA, V_HEAD), sq(V_HEAD, D)],
        out_specs=(sq(Q_LORA, KV_LORA), sq(KV_LORA, D)),
        compiler_params=_cparams(("parallel",)),
        name="mla_fold",
    )(nope, wuk, wuv, wo)


def _inproj_kernel(x_ref, mod_ref, g_ref, wa_ref, wb_ref, za_ref, zb_ref):
    h = _rms(x_ref[...], g_ref[...]) * (1.0 + mod_ref[1]) + mod_ref[0]
    hb = h.astype(BF16)
    za_ref[...] = jnp.dot(hb, wa_ref[...], preferred_element_type=F32)
    for j in range(D_ZB // D):
        zb_ref[:, j * D:(j + 1) * D] = jnp.dot(
            hb, wb_ref[:, j * D:(j + 1) * D], preferred_element_type=F32).astype(BF16)


def _inproj(x, mods, g, wa, wb, nct):
    b, nt, _ = x.shape
    return pl.pallas_call(
        _inproj_kernel,
        out_shape=(jax.ShapeDtypeStruct((b, nt, D_ZA), F32),
                   jax.ShapeDtypeStruct((b, nt, D_ZB), BF16)),
        grid=(b, nt // TM),
        in_specs=[pl.BlockSpec((None, TM, D), lambda bi, i: (bi, i, 0)),
                  pl.BlockSpec((None, None, 6, 1, D), lambda bi, i: (bi, _seg(i, nct), 0, 0, 0)),
                  pl.BlockSpec((1, D), lambda bi, i: (0, 0)),
                  pl.BlockSpec((D, D_ZA), lambda bi, i: (0, 0)),
                  pl.BlockSpec((D, D_ZB), lambda bi, i: (0, 0))],
        out_specs=(pl.BlockSpec((None, TM, D_ZA), lambda bi, i: (bi, i, 0)),
                   pl.BlockSpec((None, TM, D_ZB), lambda bi, i: (bi, i, 0))),
        compiler_params=_cparams(("parallel", "parallel"), VMEM_LIMIT),
        name="in_proj",
    )(x, mods, g, wa, wb)


def _mlaproj_kernel(za_ref, qn_ref, kvn_ref, wq_ref, cq_ref, sq_ref, ck_ref, q_ref, kv_ref, v1_ref):
    za = za_ref[...]
    cq = _rms(za[:, :Q_LORA], qn_ref[...]).astype(BF16)
    r = jnp.dot(cq, wq_ref[...], preferred_element_type=F32)
    cqt = cq_ref[...]
    sqt = sq_ref[...]
    for h in range(HEADS):
        a = r[:, h * QH:(h + 1) * QH]
        s = r[:, (HEADS + h) * QH:(HEADS + h + 1) * QH]
        q_ref[:, h * QH:(h + 1) * QH] = (a * cqt + s * sqt).astype(BF16)
    ckv = _rms(za[:, Q_LORA:Q_LORA + KV_LORA], kvn_ref[...])
    kv_ref[:, :KV_LORA] = ckv.astype(BF16)
    kv_ref[:, KV_LORA:] = (za[:, Q_LORA + KV_LORA:] * ck_ref[...]).astype(BF16)
    lane = lax.broadcasted_iota(jnp.int32, (TM, LANES), 1)
    v1_ref[:, :KV_LORA] = ckv.astype(BF16)
    v1_ref[:, KV_LORA:] = jnp.where(lane == 0, 1.0, 0.0).astype(BF16)


def _mlaproj(za, qn, kvn, wq, cq, sq, ck):
    b, nt, _ = za.shape
    return pl.pallas_call(
        _mlaproj_kernel,
        out_shape=(jax.ShapeDtypeStruct((b, nt, HEADS * QH), BF16),
                   jax.ShapeDtypeStruct((b, nt, QH), BF16),
                   jax.ShapeDtypeStruct((b, nt, 2 * LANES), BF16)),
        grid=(b, nt // TM),
        in_specs=[pl.BlockSpec((None, TM, D_ZA), lambda bi, i: (bi, i, 0)),
                  pl.BlockSpec((1, Q_LORA), lambda bi, i: (0, 0)),
                  pl.BlockSpec((1, KV_LORA), lambda bi, i: (0, 0)),
                  pl.BlockSpec((Q_LORA, 2 * HEADS * QH), lambda bi, i: (0, 0)),
                  pl.BlockSpec((TM, QH), lambda bi, i: (i, 0)),
                  pl.BlockSpec((TM, QH), lambda bi, i: (i, 0)),
                  pl.BlockSpec((TM, 2 * QK_ROPE), lambda bi, i: (i, 0))],
        out_specs=(pl.BlockSpec((None, TM, HEADS * QH), lambda bi, i: (bi, i, 0)),
                   pl.BlockSpec((None, TM, QH), lambda bi, i: (bi, i, 0)),
                   pl.BlockSpec((None, TM, 2 * LANES), lambda bi, i: (bi, i, 0))),
        compiler_params=_cparams(("parallel", "parallel"), VMEM_LIMIT),
        name="mla_proj",
    )(za, qn, kvn, wq, cq, sq, ck)


def _attn_kernel(q_ref, kv_ref, v1_ref, o_ref, q_s, m_s, acc_s, s_a, s_b, s_c,
                 *, n_ctx, n_lat, n_ctx_qsteps):
    i = pl.program_id(1)
    m_s[...] = jnp.full(m_s.shape, NEG_INF, F32)
    acc_s[...] = jnp.zeros(acc_s.shape, F32)
    for c in range(ATT_CHAINS):
        for h in range(HEADS):
            q_s[c, h * TQ:(h + 1) * TQ, :] = q_ref[c * TQ:(c + 1) * TQ, h * QH:(h + 1) * QH]

    def scores(c, s_ref, k0, size):
        k = kv_ref[pl.ds(k0, size), :]
        s_ref[c, :, :size] = lax.dot_general(q_s[c], k, (((1,), (1,)), ((), ())),
                                             preferred_element_type=F32)

    def softmax_pv(c, s_ref, k0, size):
        s = s_ref[c, :, :size]
        m_old = m_s[c]
        m_new = jnp.maximum(m_old, jnp.max(s, axis=1, keepdims=True))
        alpha = jnp.exp2(m_old - m_new)
        p = jnp.exp2(s - m_new).astype(BF16)
        acc_s[c] = alpha * acc_s[c] + jnp.dot(p, v1_ref[pl.ds(k0, size), :],
                                              preferred_element_type=F32)
        m_s[c] = m_new

    chains = range(ATT_CHAINS)

    @pl.when(i < n_ctx_qsteps)
    def _():
        for c in chains:
            scores(c, s_c, 0, n_ctx)
        for c in chains:
            softmax_pv(c, s_c, 0, n_ctx)

    tiles = [(s_c, 0, n_ctx)] + [((s_a, s_b)[t % 2], n_ctx + t * TK, TK) for t in range(n_lat // TK)]

    @pl.when(i >= n_ctx_qsteps)
    def _():
        for c in chains:
            scores(c, *tiles[0])
        for t, tile in enumerate(tiles):
            if t + 1 < len(tiles):
                for c in chains:
                    scores(c, *tiles[t + 1])
            for c in chains:
                softmax_pv(c, *tile)

    for c in chains:
        acc = acc_s[c]
        o = (acc[:, :KV_LORA] / acc[:, KV_LORA:KV_LORA + 1]).astype(BF16)
        for h in range(HEADS):
            o_ref[c * TQ:(c + 1) * TQ, h * KV_LORA:(h + 1) * KV_LORA] = o[h * TQ:(h + 1) * TQ, :]


def _attention(q, kv, v1, n_ctx):
    b, nt, _ = kv.shape
    rows = TQ * HEADS
    step = ATT_CHAINS * TQ
    n_lat = nt - n_ctx
    assert n_lat % TK == 0 and n_ctx % step == 0 and n_lat % step == 0
    kern = functools.partial(_attn_kernel, n_ctx=n_ctx, n_lat=n_lat, n_ctx_qsteps=n_ctx // step)
    return pl.pallas_call(
        kern,
        out_shape=jax.ShapeDtypeStruct((b, nt, HEADS * KV_LORA), BF16),
        grid=(b, nt // step),
        in_specs=[pl.BlockSpec((None, step, HEADS * QH), lambda bi, i: (bi, i, 0)),
                  pl.BlockSpec((None, nt, QH), lambda bi, i: (bi, 0, 0)),
                  pl.BlockSpec((None, nt, 2 * LANES), lambda bi, i: (bi, 0, 0))],
        out_specs=pl.BlockSpec((None, step, HEADS * KV_LORA), lambda bi, i: (bi, i, 0)),
        scratch_shapes=[pltpu.VMEM((ATT_CHAINS, rows, QH), BF16),
                        pltpu.VMEM((ATT_CHAINS, rows, 1), F32),
                        pltpu.VMEM((ATT_CHAINS, rows, 2 * LANES), F32),
                        pltpu.VMEM((ATT_CHAINS, rows, TK), F32),
                        pltpu.VMEM((ATT_CHAINS, rows, TK), F32),
                        pltpu.VMEM((ATT_CHAINS, rows, n_ctx), F32)],
        compiler_params=_cparams(("parallel", "arbitrary"), VMEM_LIMIT),
        name="mla_attention",
    )(q, kv, v1)


def _lru_tile(j, nct, ntl, reverse):
    if not reverse:
        return j
    return jnp.where(j < nct, nct - 1 - j, ntl - 1 - (j - nct))


def _lru_kernel(x_ref, prev_ref, next_ref, cw_ref, cb_ref, w_ref, ba_ref, bx_ref, lam_ref,
                o_ref, a_s, u_s, h_s, *, nct, ntl, reverse):
    j = pl.program_id(1)
    tile = _lru_tile(j, nct, ntl, reverse)

    @pl.when(j == 0)
    def _():
        h_s[...] = jnp.zeros(h_s.shape, F32)

    x = x_ref[...].astype(F32)
    has_prev = jnp.logical_and(tile != 0, tile != nct)
    has_next = jnp.logical_and(tile != nct - 1, tile != ntl - 1)
    prev = jnp.where(has_prev, prev_ref[...].astype(F32), 0.0)
    nxt = jnp.where(has_next, next_ref[...].astype(F32), 0.0)
    row = lax.broadcasted_iota(jnp.int32, (TM, D), 0)
    xm1 = jnp.where(row == 0, prev[HALO - 1:HALO, :], pltpu.roll(x, 1, 0))
    xm2 = jnp.where(row == 0, prev[HALO - 2:HALO - 1, :],
                    jnp.where(row == 1, prev[HALO - 1:HALO, :], pltpu.roll(x, 2, 0)))
    xp1 = jnp.where(row == TM - 1, nxt[0:1, :], pltpu.roll(x, TM - 1, 0))
    cw = cw_ref[...]
    xl = cb_ref[...] + xm2 * cw[0:1, :] + xm1 * cw[1:2, :] + x * cw[2:3, :] + xp1 * cw[3:4, :]

    z = -lam_ref[...]
    softplus = jnp.maximum(z, 0.0) + jnp.log1p(jnp.exp(-jnp.abs(z)))
    rate = -LRU_C * softplus
    for g in range(D // LANES):
        sl = slice(g * LANES, (g + 1) * LANES)
        xg = xl[:, sl]
        pre = jnp.dot(xg.astype(BF16), w_ref[g], preferred_element_type=F32)
        r = _sigmoid(pre[:, :LANES] + ba_ref[:, sl])
        ig = _sigmoid(pre[:, LANES:] + bx_ref[:, sl])
        log_a = rate[:, sl] * r
        a = jnp.exp(log_a)
        a_s[:, sl] = a
        u_s[:, sl] = jnp.sqrt(-jnp.tanh(log_a) * (1.0 + a * a)) * (ig * xg)

    srow = lax.broadcasted_iota(jnp.int32, (SUBLANES, D), 0)
    n_chunks = TM // SUBLANES

    def chunk(c, h_prev):
        cc = (n_chunks - 1 - c) if reverse else c
        r0 = pl.multiple_of(cc * SUBLANES, SUBLANES)
        a = a_s[pl.ds(r0, SUBLANES), :]
        u = u_s[pl.ds(r0, SUBLANES), :]
        for s in (1, 2, 4):
            if reverse:
                keep = srow < SUBLANES - s
                shift = SUBLANES - s
            else:
                keep = srow >= s
                shift = s
            u = u + a * jnp.where(keep, pltpu.roll(u, shift, 0), 0.0)
            a = a * jnp.where(keep, pltpu.roll(a, shift, 0), 1.0)
        h = u + a * h_prev
        o_ref[pl.ds(r0, SUBLANES), :] = h
        edge = h[0:1, :] if reverse else h[SUBLANES - 1:SUBLANES, :]
        return jnp.broadcast_to(edge, (SUBLANES, D))

    h_s[...] = lax.fori_loop(0, n_chunks, chunk, h_s[...])


def _lru(zb, cw, cb, wd, ba, bx, lam, nct, reverse):
    b, nt, _ = zb.shape
    ntl = nt // TM
    hb = TM // HALO
    n_hb = nt // HALO
    tile = lambda j: _lru_tile(j, nct, ntl, reverse)
    kern = functools.partial(_lru_kernel, nct=nct, ntl=ntl, reverse=reverse)
    vec = lambda n: pl.BlockSpec((n, D), lambda bi, j: (0, 0))
    return pl.pallas_call(
        kern,
        out_shape=jax.ShapeDtypeStruct((b, nt, D), F32),
        grid=(b, ntl),
        in_specs=[pl.BlockSpec((None, TM, D), lambda bi, j: (bi, tile(j), 0)),
                  pl.BlockSpec((None, HALO, D),
                               lambda bi, j: (bi, jnp.maximum(tile(j) * hb - 1, 0), 0)),
                  pl.BlockSpec((None, HALO, D),
                               lambda bi, j: (bi, jnp.minimum((tile(j) + 1) * hb, n_hb - 1), 0)),
                  vec(CONV_W), vec(1),
                  pl.BlockSpec((D // LANES, LANES, 2 * LANES), lambda bi, j: (0, 0, 0)),
                  vec(1), vec(1), vec(1)],
        out_specs=pl.BlockSpec((None, TM, D), lambda bi, j: (bi, tile(j), 0)),
        scratch_shapes=[pltpu.VMEM((TM, D), F32), pltpu.VMEM((TM, D), F32),
                        pltpu.VMEM((SUBLANES, D), F32)],
        compiler_params=_cparams(("arbitrary", "arbitrary"), VMEM_LIMIT),
        name="rglru_bwd" if reverse else "rglru_fwd",
    )(zb, zb, zb, cw, cb, wd, ba, bx, lam)


def _merge_kernel(o_ref, hf_ref, hb_ref, gr_ref, ga_ref, gl_ref, x_ref, mod_ref, bg_ref, nf_ref,
                  wvo_ref, wol_ref, wout_ref, wr_ref, br_ref,
                  xn_ref, h2_ref, rg_ref, rt_ref, meta_ref, cnt_ref, carry_s):
    first = jnp.logical_and(pl.program_id(0) == 0, pl.program_id(1) == 0)

    @pl.when(first)
    def _():
        carry_s[...] = jnp.zeros(carry_s.shape, F32)

    wr = wr_ref[...]
    w_hi = wr.astype(BF16)
    w_lo = (wr - w_hi.astype(F32)).astype(BF16)

    gr = gr_ref[...].astype(F32)
    gelu = 0.5 * gr * (1.0 + jnp.tanh(math.sqrt(2.0 / math.pi) * (gr + 0.044715 * (gr * gr * gr))))
    lru = ((hf_ref[...] + hb_ref[...]) * gelu).astype(BF16)
    y_att = jnp.dot(o_ref[...], wvo_ref[...], preferred_element_type=F32)
    y_lru = jnp.dot(lru, wol_ref[...], preferred_element_type=F32)
    m = (_sigmoid(ga_ref[...].astype(F32) + bg_ref[0]) * y_att
         + _sigmoid(gl_ref[...].astype(F32) + bg_ref[1]) * y_lru)
    out = jnp.dot(m.astype(BF16), wout_ref[...], preferred_element_type=F32)
    xn = x_ref[...] + mod_ref[2] * out
    xn_ref[...] = xn
    h2 = _rms(xn, nf_ref[...]) * (1.0 + mod_ref[4]) + mod_ref[3]
    h_hi = h2.astype(BF16)
    h2_ref[...] = h_hi
    h_lo = (h2 - h_hi.astype(F32)).astype(BF16)
    logits = (jnp.dot(h_hi, w_hi, preferred_element_type=F32)
              + (jnp.dot(h_lo, w_hi, preferred_element_type=F32)
                 + jnp.dot(h_hi, w_lo, preferred_element_type=F32))) + br_ref[...]
    lane = lax.broadcasted_iota(jnp.int32, (TM, LANES), 1)
    lane_f = lane.astype(F32)
    l = logits
    vals, idxs = [], []
    for _ in range(TOP_K):
        mx = jnp.max(l, axis=1, keepdims=True)
        ix = jnp.min(jnp.where(l == mx, lane_f, float(LANES)), axis=1, keepdims=True)
        vals.append(mx)
        idxs.append(ix)
        l = jnp.where(lane_f == ix, NEG_INF, l)
    chosen = jnp.where(jnp.logical_and(l == NEG_INF, lane < N_EXPERTS), 1.0, 0.0)
    es = [jnp.exp(v - vals[0]) for v in vals]
    den = es[0] + es[1] + es[2] + es[3]

    r_i = lax.broadcasted_iota(jnp.int32, (TM, TM), 0)
    c_i = lax.broadcasted_iota(jnp.int32, (TM, TM), 1)
    tri = jnp.where(r_i > c_i, 1.0, 0.0).astype(BF16)
    before = jnp.dot(tri, chosen.astype(BF16), preferred_element_type=F32)
    cnt_t = jnp.sum(chosen, axis=0, keepdims=True)
    seg_len = jnp.floor((cnt_t + (MOE_CH - 1.0)) * (1.0 / MOE_CH)) * MOE_CH
    e_r = lax.broadcasted_iota(jnp.int32, (LANES, LANES), 0)
    e_c = lax.broadcasted_iota(jnp.int32, (LANES, LANES), 1)
    upper = jnp.where(e_r < e_c, 1.0, 0.0).astype(BF16)
    seg_off = jnp.dot(jnp.broadcast_to(seg_len, (SUBLANES, LANES)).astype(BF16), upper,
                      preferred_element_type=F32)[0:1, :]
    slot_of = before + seg_off
    rg = jnp.zeros((TM, LANES), F32)
    for k in range(TOP_K):
        slot = jnp.sum(jnp.where(lane_f == idxs[k], slot_of, 0.0), axis=1, keepdims=True)
        rg = jnp.where(lane == k, es[k] / den, rg)
        rg = jnp.where(lane == TOP_K + k, slot, rg)
    rg_ref[...] = rg
    rt_ref[...] = jnp.transpose(rg)[0:SUBLANES, :]
    row8 = lax.broadcasted_iota(jnp.int32, (SUBLANES, LANES), 0)
    carry = carry_s[...]
    meta_ref[...] = jnp.where(row8 == 0, cnt_t, jnp.where(row8 == 1, carry, 0.0)).astype(jnp.int32)
    carry = carry + jnp.floor((cnt_t + (SUBLANES - 1.0)) * (1.0 / SUBLANES)) * SUBLANES
    carry_s[...] = carry
    cnt_ref[...] = jnp.broadcast_to(carry, cnt_ref.shape).astype(jnp.int32)


def _merge(o, hf, hb, zb, x, mods, bg, nf, wvo, wol, wout, wr, br, nct, tile0):
    b, nt, _ = x.shape
    ntl = nt // TM
    tok = lambda c: pl.BlockSpec((None, TM, D), lambda bi, j: (bi, j + tile0, c))
    full = lambda *s: pl.BlockSpec(s, lambda bi, j: (0,) * len(s))
    out_nt = nt - tile0 * TM
    return pl.pallas_call(
        _merge_kernel,
        out_shape=(jax.ShapeDtypeStruct((b, out_nt, D), F32),
                   jax.ShapeDtypeStruct((b, out_nt, D), BF16),
                   jax.ShapeDtypeStruct((b, out_nt, LANES), F32),
                   jax.ShapeDtypeStruct((b, out_nt // TM, SUBLANES, TM), F32),
                   jax.ShapeDtypeStruct((b, out_nt // TM, SUBLANES, LANES), jnp.int32),
                   jax.ShapeDtypeStruct((SUBLANES, LANES), jnp.int32)),
        grid=(b, ntl - tile0),
        in_specs=[tok(0), tok(0), tok(0), tok(1), tok(2), tok(3), tok(0),
                  pl.BlockSpec((None, None, 6, 1, D),
                               lambda bi, j: (bi, _seg(j + tile0, nct), 0, 0, 0)),
                  full(2, 1, D), full(1, D), full(D, D), full(D, D), full(D, D),
                  full(D, LANES), full(1, LANES)],
        out_specs=(pl.BlockSpec((None, TM, D), lambda bi, j: (bi, j, 0)),
                   pl.BlockSpec((None, TM, D), lambda bi, j: (bi, j, 0)),
                   pl.BlockSpec((None, TM, LANES), lambda bi, j: (bi, j, 0)),
                   pl.BlockSpec((None, None, SUBLANES, TM), lambda bi, j: (bi, j, 0, 0)),
                   pl.BlockSpec((None, None, SUBLANES, LANES), lambda bi, j: (bi, j, 0, 0)),
                   pl.BlockSpec((SUBLANES, LANES), lambda bi, j: (0, 0))),
        scratch_shapes=[pltpu.VMEM((1, LANES), F32)],
        compiler_params=_cparams(("arbitrary", "arbitrary"), VMEM_LIMIT),
        name="merge_route",
    )(o, hf, hb, zb, zb, zb, x, mods, bg, nf, wvo, wol, wout, wr, br)


def _dispatch_kernel(meta_ref, h2_ref, rt_ref, xs_ref, sorted_s, zero_s, sem, *, n_blk, ntl):
    bi = pl.program_id(0)
    j = pl.program_id(1)
    first = jnp.logical_and(bi == 0, j == 0)

    @pl.when(first)
    def _():
        zero_s[...] = jnp.zeros(zero_s.shape, F32)

        def pad_copy(i):
            r0 = pl.multiple_of(i * SUBLANES, SUBLANES)
            return pltpu.make_async_copy(zero_s.at[pl.ds(0, SUBLANES), :],
                                         xs_ref.at[pl.ds(r0, SUBLANES), :], sem)

        def blk_copy(i):
            r0 = pl.multiple_of(i * MOE_R, MOE_R)
            return pltpu.make_async_copy(zero_s, xs_ref.at[pl.ds(r0, MOE_R), :], sem)

        def blk_issue(i, c):
            blk_copy(i).start()
            return c

        def blk_drain(i, c):
            blk_copy(i).wait()
            return c
        n_used = meta_ref[META_NUSED]
        lax.fori_loop(n_used, n_blk, blk_issue, 0)
        lax.fori_loop(n_used, n_blk, blk_drain, 0)

        def per_expert(e, c):
            lo = meta_ref[META_TAIL + e] // SUBLANES
            hi = meta_ref[META_END + e] // SUBLANES

            def issue(r, c2):
                pad_copy(r).start()
                return c2

            def drain(r, c2):
                pad_copy(r).wait()
                return c2
            lax.fori_loop(lo, hi, issue, 0)
            lax.fori_loop(lo, hi, drain, 0)
            return c
        lax.fori_loop(0, N_EXPERTS, per_expert, 0)

    slot_i = lax.broadcasted_iota(jnp.int32, (MOE_SLOTS, TM), 0).astype(F32)
    hit = slot_i == rt_ref[TOP_K:TOP_K + 1, :]
    for k in range(1, TOP_K):
        hit = jnp.logical_or(hit, slot_i == rt_ref[TOP_K + k:TOP_K + k + 1, :])
    onehot = jnp.where(hit, 1.0, 0.0).astype(BF16)
    tile = bi * ntl + j
    cur = tile % 2
    sorted_s[cur] = jnp.dot(onehot, h2_ref[...], preferred_element_type=F32)

    def refs_of(buf):
        return lambda slot, row: (sorted_s.at[buf, pl.ds(slot, MOE_CH), :],
                                  xs_ref.at[pl.ds(row, MOE_CH), :])

    @pl.when(tile > 0)
    def _():
        _moe_chunk_copies(meta_ref, tile - 1, sem, refs_of(1 - cur), start=False)
    _moe_chunk_copies(meta_ref, tile, sem, refs_of(cur), start=True)

    @pl.when(tile == pl.num_programs(0) * ntl - 1)
    def _():
        _moe_chunk_copies(meta_ref, tile, sem, refs_of(cur), start=False)


def _moe_chunk_copies(meta_ref, tile, sem, refs, *, start):
    base = META_TILES + tile * META_PER_TILE

    def piece(c, carry):
        src, dst = refs(pl.multiple_of(meta_ref[base + 1 + 2 * c], MOE_CH),
                        pl.multiple_of(meta_ref[base + 2 + 2 * c], SUBLANES))
        cp = pltpu.make_async_copy(src, dst, sem)
        if start:
            cp.start()
        else:
            cp.wait()
        return carry
    lax.fori_loop(0, meta_ref[base], piece, 0)


def _dispatch(meta, h2, rt, n_rows):
    b, nt, _ = h2.shape
    ntl = nt // TM
    kern = functools.partial(_dispatch_kernel, n_blk=n_rows // MOE_R, ntl=ntl)
    return pl.pallas_call(
        kern,
        out_shape=jax.ShapeDtypeStruct((n_rows, D), F32),
        grid_spec=pltpu.PrefetchScalarGridSpec(
            num_scalar_prefetch=1,
            grid=(b, ntl),
            in_specs=[pl.BlockSpec((None, TM, D), lambda bi, j, meta: (bi, j, 0)),
                      pl.BlockSpec((None, None, SUBLANES, TM), lambda bi, j, meta: (bi, j, 0, 0))],
            out_specs=pl.BlockSpec(memory_space=pl.ANY),
            scratch_shapes=[pltpu.VMEM((2, MOE_SLOTS, D), F32),
                            pltpu.VMEM((MOE_R, D), F32),
                            pltpu.SemaphoreType.DMA]),
        compiler_params=_cparams(("arbitrary", "arbitrary"), VMEM_LIMIT),
        name="moe_dispatch",
    )(meta, h2, rt)


def _expert_kernel(be_ref, nu_ref, x_ref, wg_ref, bg_ref, wu_ref, bu_ref, wd_ref, bd_ref,
                   y_ref, wg_s, wu_s, wd_s):
    blk = pl.program_id(0)
    used = blk < nu_ref[0]
    prev = be_ref[jnp.maximum(blk - 1, 0)]
    fresh = jnp.logical_or(blk == 0, be_ref[blk] != prev)

    @pl.when(jnp.logical_and(used, fresh))
    def _():
        wg_s[...] = wg_ref[...].astype(BF16)
        wu_s[...] = wu_ref[...].astype(BF16)
        wd_s[...] = wd_ref[...].astype(BF16)

    @pl.when(used)
    def _():
        x = x_ref[...].astype(BF16)
        g = jnp.dot(x, wg_s[...], preferred_element_type=F32) + bg_ref[...]
        u = jnp.dot(x, wu_s[...], preferred_element_type=F32) + bu_ref[...]
        g = jnp.minimum(g, SWIGLU_LIMIT)
        u = jnp.clip(u, -SWIGLU_LIMIT, SWIGLU_LIMIT)
        act = g * _sigmoid(SWIGLU_ALPHA * g) * (u + 1.0)
        y_ref[...] = jnp.dot(act.astype(BF16), wd_s[...], preferred_element_type=F32) + bd_ref[...]

    @pl.when(jnp.logical_not(used))
    def _():
        y_ref[...] = jnp.zeros(y_ref.shape, F32)


def _experts(blk_expert, n_used, xs, wg, bg, wu, bu, wd, bd, layer):
    n_rows = xs.shape[0]
    n_blk = n_rows // MOE_R
    dff = wg.shape[-1]
    e_of = lambda i, be, nu: be[jnp.minimum(i, nu[0] - 1)]
    wspec = lambda r, c: pl.BlockSpec((None, None, r, c), lambda i, be, nu: (layer, e_of(i, be, nu), 0, 0))
    return pl.pallas_call(
        _expert_kernel,
        out_shape=jax.ShapeDtypeStruct((n_rows, D), F32),
        grid_spec=pltpu.PrefetchScalarGridSpec(
            num_scalar_prefetch=2,
            grid=(n_blk,),
            in_specs=[pl.BlockSpec((MOE_R, D), lambda i, be, nu: (jnp.minimum(i, nu[0] - 1), 0)),
                      wspec(D, dff), wspec(1, dff), wspec(D, dff), wspec(1, dff),
                      wspec(dff, D), wspec(1, D)],
            out_specs=pl.BlockSpec((MOE_R, D), lambda i, be, nu: (i, 0)),
            scratch_shapes=[pltpu.VMEM((D, dff), BF16), pltpu.VMEM((D, dff), BF16),
                            pltpu.VMEM((dff, D), BF16)]),
        compiler_params=_cparams(("arbitrary",), VMEM_LIMIT),
        name="moe_experts",
    )(blk_expert, n_used, xs, wg, bg, wu, bu, wd, bd)


def _combine_kernel(meta_ref, x_ref, rg_ref, mod_ref, fn_ref, ys_ref, o_ref, buf_s, sem, *, final, ntl):
    bi = pl.program_id(0)
    j = pl.program_id(1)

    tile = bi * ntl + j
    cur = tile % 2

    def refs_of(buf):
        return lambda slot, row: (ys_ref.at[pl.ds(row, MOE_CH), :],
                                  buf_s.at[buf, pl.ds(slot, MOE_CH), :])

    @pl.when(tile == 0)
    def _():
        buf_s[...] = jnp.zeros(buf_s.shape, F32)
        _moe_chunk_copies(meta_ref, tile, sem, refs_of(cur), start=True)

    _moe_chunk_copies(meta_ref, tile, sem, refs_of(cur), start=False)

    @pl.when(tile < pl.num_programs(0) * ntl - 1)
    def _():
        _moe_chunk_copies(meta_ref, tile + 1, sem, refs_of(1 - cur), start=True)

    rg = rg_ref[...]
    slot_i = lax.broadcasted_iota(jnp.int32, (TM, MOE_SLOTS), 1).astype(F32)
    gate = jnp.zeros((TM, MOE_SLOTS), F32)
    for k in range(TOP_K):
        gate = jnp.where(slot_i == rg[:, TOP_K + k:TOP_K + k + 1], rg[:, k:k + 1], gate)
    y = jnp.dot(gate.astype(BF16), buf_s[cur].astype(BF16), preferred_element_type=F32)
    xo = x_ref[...] + mod_ref[5] * y
    if final:
        xo = _rms(xo, fn_ref[...])
    o_ref[...] = xo


def _combine(meta, x, rg, mods, fn, ys, nct, final):
    b, nt, _ = x.shape
    ntl = nt // TM
    kern = functools.partial(_combine_kernel, final=final, ntl=ntl)
    return pl.pallas_call(
        kern,
        out_shape=jax.ShapeDtypeStruct((b, nt, D), F32),
        grid_spec=pltpu.PrefetchScalarGridSpec(
            num_scalar_prefetch=1,
            grid=(b, ntl),
            in_specs=[pl.BlockSpec((None, TM, D), lambda bi, j, meta: (bi, j, 0)),
                      pl.BlockSpec((None, TM, LANES), lambda bi, j, meta: (bi, j, 0)),
                      pl.BlockSpec((None, None, 6, 1, D),
                                   lambda bi, j, meta: (bi, _seg(j, nct), 0, 0, 0)),
                      pl.BlockSpec((1, D), lambda bi, j, meta: (0, 0)),
                      pl.BlockSpec(memory_space=pl.ANY)],
            out_specs=pl.BlockSpec((None, TM, D), lambda bi, j, meta: (bi, j, 0)),
            scratch_shapes=[pltpu.VMEM((2, MOE_SLOTS, D), F32), pltpu.SemaphoreType.DMA]),
        compiler_params=_cparams(("arbitrary", "arbitrary"), VMEM_LIMIT),
        name="moe_combine",
    )(meta, x, rg, mods, fn, ys)


_ROT_PARTNER = tuple(list(range(16, 32)) + list(range(0, 16)) + list(range(48, 64)) + list(range(32, 48)))


def _rope_tables(n_ctx, n_lat):
    n = jnp.arange(n_lat)
    row = (n // GRID_W).astype(F32)
    col = (n % GRID_W).astype(F32)
    n_freq = QK_ROPE // 4
    inv = ROPE_THETA ** (-jnp.arange(n_freq, dtype=F32) / n_freq)
    ar = row[:, None] * inv
    ac = col[:, None] * inv
    c64 = jnp.concatenate([jnp.cos(ar), jnp.cos(ar), jnp.cos(ac), jnp.cos(ac)], axis=1)
    s64 = jnp.concatenate([-jnp.sin(ar), jnp.sin(ar), -jnp.sin(ac), jnp.sin(ac)], axis=1)
    c64 = jnp.concatenate([jnp.ones((n_ctx, QK_ROPE), F32), c64], axis=0)
    s64 = jnp.concatenate([jnp.zeros((n_ctx, QK_ROPE), F32), s64], axis=0)
    nt = n_ctx + n_lat
    qs = ATTN_SCALE * LOG2E
    cq = qs * jnp.concatenate([jnp.ones((nt, QK_NOPE), F32), c64, c64], axis=1)
    sq = qs * jnp.concatenate([jnp.zeros((nt, QK_NOPE), F32), s64, s64], axis=1)
    ck = jnp.concatenate([c64, s64], axis=1)
    return cq, sq, ck


def _block_diag_pairs(w):
    w = w.reshape(LRU_BLOCKS // 2, 2, LRU_BW, LRU_BW)
    z = jnp.zeros((LRU_BLOCKS // 2, LRU_BW, LRU_BW), w.dtype)
    top = jnp.concatenate([w[:, 0], z], axis=2)
    bot = jnp.concatenate([z, w[:, 1]], axis=2)
    return jnp.concatenate([top, bot], axis=1)


def kernel(x, c, ctx, c_ctx, w_ada, b_ada, norm_mix, norm_ffn, w_in, b_branch_gate, q_norm, w_uq, kv_norm, w_ukv, w_o_attn, conv_w, conv_b, lru_w_a, lru_b_a, lru_w_x, lru_b_x, lru_lambda, w_o_lru, w_out, w_router, b_router, w_exp_gate, b_exp_gate, w_exp_up, b_exp_up, w_exp_down, b_exp_down, final_norm):
    depth = w_in.shape[0]
    b, n_lat, _ = x.shape
    n_ctx = ctx.shape[1]
    assert n_ctx % TM == 0 and n_lat % TK == 0 and b + 1 <= SUBLANES
    nct = n_ctx // TM
    nt = n_ctx + n_lat
    ntl = nt // TM
    perm = jnp.array(_ROT_PARTNER)

    xs = jnp.concatenate([ctx, x], axis=1)
    cond8 = jnp.zeros((SUBLANES, D), F32).at[:b].set(c).at[b].set(c_ctx)
    cq_t, sq_t, ck_t = _rope_tables(n_ctx, n_lat)

    for l in range(depth):
        last = l == depth - 1
        tile0 = nct if last else 0

        m8 = _ada(cond8, w_ada[l], b_ada[l][None, :]).reshape(SUBLANES, 6, 1, D)
        mods = jnp.stack([jnp.broadcast_to(m8[b], (b,) + m8.shape[1:]), m8[:b]], axis=1)

        wi = w_in[l]
        kpe_w = wi[:, Q_LORA + KV_LORA:Q_LORA + KV_LORA + QK_ROPE]
        wa = jnp.concatenate([wi[:, :Q_LORA + KV_LORA + QK_ROPE], kpe_w[:, perm]], axis=1).astype(BF16)
        wb = wi[:, Q_LORA + KV_LORA + QK_ROPE:].astype(BF16)
        uq = w_uq[l].reshape(Q_LORA, HEADS, QK_DIM).transpose(1, 0, 2)
        ukv = w_ukv[l].reshape(KV_LORA, HEADS, QK_NOPE + V_HEAD).transpose(1, 0, 2)
        wqa, wvo = _fold(uq[:, :, :QK_NOPE], ukv[:, :, :QK_NOPE], ukv[:, :, QK_NOPE:],
                         w_o_attn[l].reshape(HEADS, V_HEAD, D))
        pe = uq[:, :, QK_NOPE:]
        pe_sw = pe[:, :, perm]
        zq = jnp.zeros((HEADS, Q_LORA, QK_NOPE), F32)
        wq1 = jnp.concatenate([wqa, pe, pe], axis=2)
        wq2 = jnp.concatenate([zq, pe_sw, pe_sw], axis=2)
        wq = jnp.concatenate([wq1.transpose(1, 0, 2).reshape(Q_LORA, HEADS * QH),
                              wq2.transpose(1, 0, 2).reshape(Q_LORA, HEADS * QH)], axis=1).astype(BF16)
        wvo = wvo.reshape(D, D).astype(BF16)
        wr = jnp.zeros((D, LANES), F32).at[:, :N_EXPERTS].set(w_router[l])
        br = jnp.full((1, LANES), NEG_INF, F32).at[0, :N_EXPERTS].set(b_router[l])

        za, zb = _inproj(xs, mods, norm_mix[l][None, :], wa, wb, nct)
        q, kv, v1 = _mlaproj(za, q_norm[l][None, :], kv_norm[l][None, :], wq, cq_t, sq_t, ck_t)
        o = _attention(q, kv, v1, n_ctx)
        hs = []
        for d in range(2):
            wd = jnp.concatenate([_block_diag_pairs(lru_w_a[l, d]), _block_diag_pairs(lru_w_x[l, d])],
                                 axis=2).astype(BF16)
            hs.append(_lru(zb, conv_w[l], conv_b[l][None, :], wd, lru_b_a[l, d][None, :],
                           lru_b_x[l, d][None, :], lru_lambda[l, d][None, :], nct, reverse=d == 1))
        xn, h2, rg, rt, tile_meta, cnt = _merge(
            o, hs[0], hs[1], zb, xs, mods, b_branch_gate[l].reshape(2, 1, D), norm_ffn[l][None, :],
            wvo, w_o_lru[l].astype(BF16), w_out[l].astype(BF16), wr, br, nct, tile0)

        counts = cnt[0, :N_EXPERTS]
        padded = ((counts + (MOE_CH - 1) + MOE_R - 1) // MOE_R) * MOE_R
        pad_end = jnp.cumsum(padded)
        pad_start = pad_end - padded
        moe_nt = xn.shape[1]
        n_tiles = b * moe_nt // TM
        max_rows = (b * moe_nt * TOP_K + n_tiles * N_EXPERTS * (SUBLANES - 1)
                    + N_EXPERTS * (MOE_CH - 1))
        n_blk = max_rows // MOE_R + N_EXPERTS
        n_rows = n_blk * MOE_R
        blk_start = jnp.arange(n_blk, dtype=jnp.int32) * MOE_R
        blk_expert = jnp.minimum(jnp.sum(blk_start[:, None] >= pad_end[None, :], axis=1),
                                 N_EXPERTS - 1).astype(jnp.int32)
        n_used = (pad_end[-1:] // MOE_R).astype(jnp.int32)
        cnt_t = tile_meta[:, :, 0, :N_EXPERTS].reshape(n_tiles, N_EXPERTS)
        row0_t = pad_start[None, :] + tile_meta[:, :, 1, :N_EXPERTS].reshape(n_tiles, N_EXPERTS)
        n_ch = (cnt_t + MOE_CH - 1) // MOE_CH
        ch_end = jnp.cumsum(n_ch, axis=1)
        ch_start = ch_end - n_ch
        c_ids = jnp.arange(MOE_MAX_COPIES, dtype=jnp.int32)
        e_of_c = jnp.minimum(jnp.sum(c_ids[None, :, None] >= ch_end[:, None, :], axis=-1), N_EXPERTS - 1)
        is_e = e_of_c[:, :, None] == jnp.arange(N_EXPERTS, dtype=jnp.int32)
        pick = lambda v: jnp.sum(jnp.where(is_e, v[:, None, :], 0), axis=-1)
        first_c = pick(ch_start)
        piece = (c_ids[None, :] - first_c) * MOE_CH
        copies = jnp.stack([first_c * MOE_CH + piece, pick(row0_t) + piece], axis=-1)
        tile_tab = jnp.concatenate(
            [ch_end[:, -1:], copies.reshape(n_tiles, 2 * MOE_MAX_COPIES),
             jnp.zeros((n_tiles, META_PER_TILE - 1 - 2 * MOE_MAX_COPIES), jnp.int32)], axis=1)
        meta = jnp.concatenate([pad_start, pad_start + counts, pad_end,
                                jnp.broadcast_to(n_used, (N_EXPERTS,)),
                                tile_tab.reshape(-1)]).astype(jnp.int32)

        xsort = _dispatch(meta, h2, rt, n_rows)
        ys = _experts(blk_expert, n_used, xsort, w_exp_gate, b_exp_gate[:, :, None, :],
                      w_exp_up, b_exp_up[:, :, None, :], w_exp_down, b_exp_down[:, :, None, :], l)
        xs = _combine(meta, xn, rg, mods, final_norm[None, :], ys, nct - tile0, last)
    return xs
```

```python
import functools
import math

import jax
import jax.numpy as jnp
from jax import lax
from jax.experimental import pallas as pl
from jax.experimental.pallas import tpu as pltpu

D = 1024
GRID_W = 64
HEADS = 8
QK_NOPE = 128
QK_ROPE = 64
QK_DIM = QK_NOPE + QK_ROPE
V_HEAD = 128
Q_LORA = 256
KV_LORA = 128
ROPE_THETA = 10000.0
ATTN_SCALE = QK_DIM ** -0.5
LRU_BLOCKS = 16
LRU_BW = D // LRU_BLOCKS
CONV_W = 4
LRU_C = 8.0
N_EXPERTS = 32
TOP_K = 4
SWIGLU_ALPHA = 1.702
SWIGLU_LIMIT = 7.0
NORM_EPS = 1e-6
D_ZA = Q_LORA + KV_LORA + 2 * QK_ROPE
D_ZB = 4 * D
QH = 256

LANES = 128
SUBLANES = 8
TM = 256
HALO = 16
TQ = 128
ATT_CHAINS = 1
TK = 2048
MOE_R = 512
MOE_CH = 16
MOE_SLOTS = TM * TOP_K + N_EXPERTS * MOE_CH
META_TAIL = N_EXPERTS
META_END = 2 * N_EXPERTS
META_NUSED = 3 * N_EXPERTS
META_TILES = 4 * N_EXPERTS
MOE_MAX_COPIES = MOE_SLOTS // MOE_CH
META_PER_TILE = 256
assert 1 + 2 * MOE_MAX_COPIES <= META_PER_TILE
VMEM_LIMIT = 56 * 1024 * 1024

F32 = jnp.float32
BF16 = jnp.bfloat16
NEG_INF = float("-inf")
LOG2E = 1.4426950408889634


def _cparams(sem, vmem=None):
    return pltpu.CompilerParams(dimension_semantics=sem, vmem_limit_bytes=vmem)


def _rms(x, g):
    return x * lax.rsqrt(jnp.mean(x * x, axis=-1, keepdims=True) + NORM_EPS) * g


def _sigmoid(x):
    return 0.5 * jnp.tanh(0.5 * x) + 0.5


def _seg(i, nct):
    return jnp.where(i < nct, 0, 1)


def _ada_kernel(c_ref, w_ref, b_ref, o_ref):
    c = c_ref[...]
    s = (c * _sigmoid(c)).astype(BF16)
    o_ref[...] = jnp.dot(s, w_ref[...].astype(BF16), preferred_element_type=F32) + b_ref[...]


def _ada(cond8, w, b, tn=768):
    n = w.shape[1]
    return pl.pallas_call(
        _ada_kernel,
        out_shape=jax.ShapeDtypeStruct((SUBLANES, n), F32),
        grid=(n // tn,),
        in_specs=[pl.BlockSpec((SUBLANES, D), lambda j: (0, 0)),
                  pl.BlockSpec((D, tn), lambda j: (0, j)),
                  pl.BlockSpec((1, tn), lambda j: (0, j))],
        out_specs=pl.BlockSpec((SUBLANES, tn), lambda j: (0, j)),
        compiler_params=_cparams(("parallel",)),
        name="adaln",
    )(cond8, w, b)


def _fold_kernel(nope_ref, wuk_ref, wuv_ref, wo_ref, wqa_ref, wvo_ref):
    hp = lax.Precision.HIGHEST
    wqa_ref[...] = lax.dot_general(nope_ref[...], wuk_ref[...], (((1,), (1,)), ((), ())),
                                   precision=hp, preferred_element_type=F32)
    wvo_ref[...] = jnp.dot(wuv_ref[...], wo_ref[...], precision=hp, preferred_element_type=F32)


def _fold(nope, wuk, wuv, wo):
    sq = lambda *s: pl.BlockSpec((None,) + s, lambda h: (h, 0, 0))
    return pl.pallas_call(
        _fold_kernel,
        out_shape=(jax.ShapeDtypeStruct((HEADS, Q_LORA, KV_LORA), F32),
                   jax.ShapeDtypeStruct((HEADS, KV_LORA, D), F32)),
        grid=(HEADS,),
        in_specs=[sq(Q_LORA, QK_NOPE), sq(KV_LORA, QK_NOPE), sq(KV_LORA, V_HEAD), sq(V_HEAD, D)],
        out_specs=(sq(Q_LORA, KV_LORA), sq(KV_LORA, D)),
        compiler_params=_cparams(("parallel",)),
        name="mla_fold",
    )(nope, wuk, wuv, wo)


def _inproj_kernel(x_ref, mod_ref, g_ref, wa_ref, wb_ref, za_ref, zb_ref):
    h = _rms(x_ref[...], g_ref[...]) * (1.0 + mod_ref[1]) + mod_ref[0]
    hb = h.astype(BF16)
    za_ref[...] = jnp.dot(hb, wa_ref[...], preferred_element_type=F32)
    for j in range(D_ZB // D):
        zb_ref[:, j * D:(j + 1) * D] = jnp.dot(
            hb, wb_ref[:, j * D:(j + 1) * D], preferred_element_type=F32).astype(BF16)


def _inproj(x, mods, g, wa, wb, nct):
    b, nt, _ = x.shape
    return pl.pallas_call(
        _inproj_kernel,
        out_shape=(jax.ShapeDtypeStruct((b, nt, D_ZA), F32),
                   jax.ShapeDtypeStruct((b, nt, D_ZB), BF16)),
        grid=(b, nt // TM),
        in_specs=[pl.BlockSpec((None, TM, D), lambda bi, i: (bi, i, 0)),
                  pl.BlockSpec((None, None, 6, 1, D), lambda bi, i: (bi, _seg(i, nct), 0, 0, 0)),
                  pl.BlockSpec((1, D), lambda bi, i: (0, 0)),
                  pl.BlockSpec((D, D_ZA), lambda bi, i: (0, 0)),
                  pl.BlockSpec((D, D_ZB), lambda bi, i: (0, 0))],
        out_specs=(pl.BlockSpec((None, TM, D_ZA), lambda bi, i: (bi, i, 0)),
                   pl.BlockSpec((None, TM, D_ZB), lambda bi, i: (bi, i, 0))),
        compiler_params=_cparams(("parallel", "parallel"), VMEM_LIMIT),
        name="in_proj",
    )(x, mods, g, wa, wb)


def _mlaproj_kernel(za_ref, qn_ref, kvn_ref, wq_ref, cq_ref, sq_ref, ck_ref, q_ref, kv_ref, v1_ref):
    za = za_ref[...]
    cq = _rms(za[:, :Q_LORA], qn_ref[...]).astype(BF16)
    r = jnp.dot(cq, wq_ref[...], preferred_element_type=F32)
    cqt = cq_ref[...]
    sqt = sq_ref[...]
    for h in range(HEADS):
        a = r[:, h * QH:(h + 1) * QH]
        s = r[:, (HEADS + h) * QH:(HEADS + h + 1) * QH]
        q_ref[:, h * QH:(h + 1) * QH] = (a * cqt + s * sqt).astype(BF16)
    ckv = _rms(za[:, Q_LORA:Q_LORA + KV_LORA], kvn_ref[...])
    kv_ref[:, :KV_LORA] = ckv.astype(BF16)
    kv_ref[:, KV_LORA:] = (za[:, Q_LORA + KV_LORA:] * ck_ref[...]).astype(BF16)
    lane = lax.broadcasted_iota(jnp.int32, (TM, LANES), 1)
    v1_ref[:, :KV_LORA] = ckv.astype(BF16)
    v1_ref[:, KV_LORA:] = jnp.where(lane == 0, 1.0, 0.0).astype(BF16)


def _mlaproj(za, qn, kvn, wq, cq, sq, ck):
    b, nt, _ = za.shape
    return pl.pallas_call(
        _mlaproj_kernel,
        out_shape=(jax.ShapeDtypeStruct((b, nt, HEADS * QH), BF16),
                   jax.ShapeDtypeStruct((b, nt, QH), BF16),
                   jax.ShapeDtypeStruct((b, nt, 2 * LANES), BF16)),
        grid=(b, nt // TM),
        in_specs=[pl.BlockSpec((None, TM, D_ZA), lambda bi, i: (bi, i, 0)),
                  pl.BlockSpec((1, Q_LORA), lambda bi, i: (0, 0)),
                  pl.BlockSpec((1, KV_LORA), lambda bi, i: (0, 0)),
                  pl.BlockSpec((Q_LORA, 2 * HEADS * QH), lambda bi, i: (0, 0)),
                  pl.BlockSpec((TM, QH), lambda bi, i: (i, 0)),
                  pl.BlockSpec((TM, QH), lambda bi, i: (i, 0)),
                  pl.BlockSpec((TM, 2 * QK_ROPE), lambda bi, i: (i, 0))],
        out_specs=(pl.BlockSpec((None, TM, HEADS * QH), lambda bi, i: (bi, i, 0)),
                   pl.BlockSpec((None, TM, QH), lambda bi, i: (bi, i, 0)),
                   pl.BlockSpec((None, TM, 2 * LANES), lambda bi, i: (bi, i, 0))),
        compiler_params=_cparams(("parallel", "parallel"), VMEM_LIMIT),
        name="mla_proj",
    )(za, qn, kvn, wq, cq, sq, ck)


def _attn_kernel(q_ref, kv_ref, v1_ref, o_ref, q_s, m_s, acc_s, s_a, s_b, s_c,
                 *, n_ctx, n_lat, n_ctx_qsteps):
    i = pl.program_id(1)
    m_s[...] = jnp.full(m_s.shape, NEG_INF, F32)
    acc_s[...] = jnp.zeros(acc_s.shape, F32)
    for c in range(ATT_CHAINS):
        for h in range(HEADS):
            q_s[c, h * TQ:(h + 1) * TQ, :] = q_ref[c * TQ:(c + 1) * TQ, h * QH:(h + 1) * QH]

    def scores(c, s_ref, k0, size):
        k = kv_ref[pl.ds(k0, size), :]
        s_ref[c, :, :size] = lax.dot_general(q_s[c], k, (((1,), (1,)), ((), ())),
                                             preferred_element_type=F32)

    def softmax_pv(c, s_ref, k0, size):
        s = s_ref[c, :, :size]
        m_old = m_s[c]
        m_new = jnp.maximum(m_old, jnp.max(s, axis=1, keepdims=True))
        alpha = jnp.exp2(m_old - m_new)
        p = jnp.exp2(s - m_new).astype(BF16)
        acc_s[c] = alpha * acc_s[c] + jnp.dot(p, v1_ref[pl.ds(k0, size), :],
                                              preferred_element_type=F32)
        m_s[c] = m_new

    chains = range(ATT_CHAINS)

    @pl.when(i < n_ctx_qsteps)
    def _():
        for c in chains:
            scores(c, s_c, 0, n_ctx)
        for c in chains:
            softmax_pv(c, s_c, 0, n_ctx)

    tiles = [(s_c, 0, n_ctx)] + [((s_a, s_b)[t % 2], n_ctx + t * TK, TK) for t in range(n_lat // TK)]

    @pl.when(i >= n_ctx_qsteps)
    def _():
        for c in chains:
            scores(c, *tiles[0])
        for t, tile in enumerate(tiles):
            if t + 1 < len(tiles):
                for c in chains:
                    scores(c, *tiles[t + 1])
            for c in chains:
                softmax_pv(c, *tile)

    for c in chains:
        acc = acc_s[c]
        o = (acc[:, :KV_LORA] / acc[:, KV_LORA:KV_LORA + 1]).astype(BF16)
        for h in range(HEADS):
            o_ref[c * TQ:(c + 1) * TQ, h * KV_LORA:(h + 1) * KV_LORA] = o[h * TQ:(h + 1) * TQ, :]


def _attention(q, kv, v1, n_ctx):
    b, nt, _ = kv.shape
    rows = TQ * HEADS
    step = ATT_CHAINS * TQ
    n_lat = nt - n_ctx
    assert n_lat % TK == 0 and n_ctx % step == 0 and n_lat % step == 0
    kern = functools.partial(_attn_kernel, n_ctx=n_ctx, n_lat=n_lat, n_ctx_qsteps=n_ctx // step)
    return pl.pallas_call(
        kern,
        out_shape=jax.ShapeDtypeStruct((b, nt, HEADS * KV_LORA), BF16),
        grid=(b, nt // step),
        in_specs=[pl.BlockSpec((None, step, HEADS * QH), lambda bi, i: (bi, i, 0)),
                  pl.BlockSpec((None, nt, QH), lambda bi, i: (bi, 0, 0)),
                  pl.BlockSpec((None, nt, 2 * LANES), lambda bi, i: (bi, 0, 0))],
        out_specs=pl.BlockSpec((None, step, HEADS * KV_LORA), lambda bi, i: (bi, i, 0)),
        scratch_shapes=[pltpu.VMEM((ATT_CHAINS, rows, QH), BF16),
                        pltpu.VMEM((ATT_CHAINS, rows, 1), F32),
                        pltpu.VMEM((ATT_CHAINS, rows, 2 * LANES), F32),
                        pltpu.VMEM((ATT_CHAINS, rows, TK), F32),
                        pltpu.VMEM((ATT_CHAINS, rows, TK), F32),
                        pltpu.VMEM((ATT_CHAINS, rows, n_ctx), F32)],
        compiler_params=_cparams(("parallel", "arbitrary"), VMEM_LIMIT),
        name="mla_attention",
    )(q, kv, v1)


def _lru_tile(j, nct, ntl, reverse):
    if not reverse:
        return j
    return jnp.where(j < nct, nct - 1 - j, ntl - 1 - (j - nct))


def _lru_kernel(x_ref, prev_ref, next_ref, cw_ref, cb_ref, w_ref, ba_ref, bx_ref, lam_ref,
                o_ref, a_s, u_s, h_s, *, nct, ntl, reverse):
    j = pl.program_id(1)
    tile = _lru_tile(j, nct, ntl, reverse)

    @pl.when(j == 0)
    def _():
        h_s[...] = jnp.zeros(h_s.shape, F32)

    x = x_ref[...].astype(F32)
    has_prev = jnp.logical_and(tile != 0, tile != nct)
    has_next = jnp.logical_and(tile != nct - 1, tile != ntl - 1)
    prev = jnp.where(has_prev, prev_ref[...].astype(F32), 0.0)
    nxt = jnp.where(has_next, next_ref[...].astype(F32), 0.0)
    row = lax.broadcasted_iota(jnp.int32, (TM, D), 0)
    xm1 = jnp.where(row == 0, prev[HALO - 1:HALO, :], pltpu.roll(x, 1, 0))
    xm2 = jnp.where(row == 0, prev[HALO - 2:HALO - 1, :],
                    jnp.where(row == 1, prev[HALO - 1:HALO, :], pltpu.roll(x, 2, 0)))
    xp1 = jnp.where(row == TM - 1, nxt[0:1, :], pltpu.roll(x, TM - 1, 0))
    cw = cw_ref[...]
    xl = cb_ref[...] + xm2 * cw[0:1, :] + xm1 * cw[1:2, :] + x * cw[2:3, :] + xp1 * cw[3:4, :]

    z = -lam_ref[...]
    softplus = jnp.maximum(z, 0.0) + jnp.log1p(jnp.exp(-jnp.abs(z)))
    rate = -LRU_C * softplus
    for g in range(D // LANES):
        sl = slice(g * LANES, (g + 1) * LANES)
        xg = xl[:, sl]
        pre = jnp.dot(xg.astype(BF16), w_ref[g], preferred_element_type=F32)
        r = _sigmoid(pre[:, :LANES] + ba_ref[:, sl])
        ig = _sigmoid(pre[:, LANES:] + bx_ref[:, sl])
        log_a = rate[:, sl] * r
        a = jnp.exp(log_a)
        a_s[:, sl] = a
        u_s[:, sl] = jnp.sqrt(-jnp.tanh(log_a) * (1.0 + a * a)) * (ig * xg)

    srow = lax.broadcasted_iota(jnp.int32, (SUBLANES, D), 0)
    n_chunks = TM // SUBLANES

    def chunk(c, h_prev):
        cc = (n_chunks - 1 - c) if reverse else c
        r0 = pl.multiple_of(cc * SUBLANES, SUBLANES)
        a = a_s[pl.ds(r0, SUBLANES), :]
        u = u_s[pl.ds(r0, SUBLANES), :]
        for s in (1, 2, 4):
            if reverse:
                keep = srow < SUBLANES - s
                shift = SUBLANES - s
            else:
                keep = srow >= s
                shift = s
            u = u + a * jnp.where(keep, pltpu.roll(u, shift, 0), 0.0)
            a = a * jnp.where(keep, pltpu.roll(a, shift, 0), 1.0)
        h = u + a * h_prev
        o_ref[pl.ds(r0, SUBLANES), :] = h
        edge = h[0:1, :] if reverse else h[SUBLANES - 1:SUBLANES, :]
        return jnp.broadcast_to(edge, (SUBLANES, D))

    h_s[...] = lax.fori_loop(0, n_chunks, chunk, h_s[...])


def _lru(zb, cw, cb, wd, ba, bx, lam, nct, reverse):
    b, nt, _ = zb.shape
    ntl = nt // TM
    hb = TM // HALO
    n_hb = nt // HALO
    tile = lambda j: _lru_tile(j, nct, ntl, reverse)
    kern = functools.partial(_lru_kernel, nct=nct, ntl=ntl, reverse=reverse)
    vec = lambda n: pl.BlockSpec((n, D), lambda bi, j: (0, 0))
    return pl.pallas_call(
        kern,
        out_shape=jax.ShapeDtypeStruct((b, nt, D), F32),
        grid=(b, ntl),
        in_specs=[pl.BlockSpec((None, TM, D), lambda bi, j: (bi, tile(j), 0)),
                  pl.BlockSpec((None, HALO, D),
                               lambda bi, j: (bi, jnp.maximum(tile(j) * hb - 1, 0), 0)),
                  pl.BlockSpec((None, HALO, D),
                               lambda bi, j: (bi, jnp.minimum((tile(j) + 1) * hb, n_hb - 1), 0)),
                  vec(CONV_W), vec(1),
                  pl.BlockSpec((D // LANES, LANES, 2 * LANES), lambda bi, j: (0, 0, 0)),
                  vec(1), vec(1), vec(1)],
        out_specs=pl.BlockSpec((None, TM, D), lambda bi, j: (bi, tile(j), 0)),
        scratch_shapes=[pltpu.VMEM((TM, D), F32), pltpu.VMEM((TM, D), F32),
                        pltpu.VMEM((SUBLANES, D), F32)],
        compiler_params=_cparams(("arbitrary", "arbitrary"), VMEM_LIMIT),
        name="rglru_bwd" if reverse else "rglru_fwd",
    )(zb, zb, zb, cw, cb, wd, ba, bx, lam)


def _merge_kernel(o_ref, hf_ref, hb_ref, gr_ref, ga_ref, gl_ref, x_ref, mod_ref, bg_ref, nf_ref,
                  wvo_ref, wol_ref, wout_ref, wr_ref, br_ref,
                  xn_ref, h2_ref, rg_ref, rt_ref, meta_ref, cnt_ref, carry_s):
    first = jnp.logical_and(pl.program_id(0) == 0, pl.program_id(1) == 0)

    @pl.when(first)
    def _():
        carry_s[...] = jnp.zeros(carry_s.shape, F32)

    wr = wr_ref[...]
    w_hi = wr.astype(BF16)
    w_lo = (wr - w_hi.astype(F32)).astype(BF16)

    gr = gr_ref[...].astype(F32)
    gelu = 0.5 * gr * (1.0 + jnp.tanh(math.sqrt(2.0 / math.pi) * (gr + 0.044715 * (gr * gr * gr))))
    lru = ((hf_ref[...] + hb_ref[...]) * gelu).astype(BF16)
    y_att = jnp.dot(o_ref[...], wvo_ref[...], preferred_element_type=F32)
    y_lru = jnp.dot(lru, wol_ref[...], preferred_element_type=F32)
    m = (_sigmoid(ga_ref[...].astype(F32) + bg_ref[0]) * y_att
         + _sigmoid(gl_ref[...].astype(F32) + bg_ref[1]) * y_lru)
    out = jnp.dot(m.astype(BF16), wout_ref[...], preferred_element_type=F32)
    xn = x_ref[...] + mod_ref[2] * out
    xn_ref[...] = xn
    h2 = _rms(xn, nf_ref[...]) * (1.0 + mod_ref[4]) + mod_ref[3]
    h_hi = h2.astype(BF16)
    h2_ref[...] = h_hi
    h_lo = (h2 - h_hi.astype(F32)).astype(BF16)
    logits = (jnp.dot(h_hi, w_hi, preferred_element_type=F32)
              + (jnp.dot(h_lo, w_hi, preferred_element_type=F32)
                 + jnp.dot(h_hi, w_lo, preferred_element_type=F32))) + br_ref[...]
    lane = lax.broadcasted_iota(jnp.int32, (TM, LANES), 1)
    lane_f = lane.astype(F32)
    l = logits
    vals, idxs = [], []
    for _ in range(TOP_K):
        mx = jnp.max(l, axis=1, keepdims=True)
        ix = jnp.min(jnp.where(l == mx, lane_f, float(LANES)), axis=1, keepdims=True)
        vals.append(mx)
        idxs.append(ix)
        l = jnp.where(lane_f == ix, NEG_INF, l)
    chosen = jnp.where(jnp.logical_and(l == NEG_INF, lane < N_EXPERTS), 1.0, 0.0)
    es = [jnp.exp(v - vals[0]) for v in vals]
    den = es[0] + es[1] + es[2] + es[3]

    r_i = lax.broadcasted_iota(jnp.int32, (TM, TM), 0)
    c_i = lax.broadcasted_iota(jnp.int32, (TM, TM), 1)
    tri = jnp.where(r_i > c_i, 1.0, 0.0).astype(BF16)
    before = jnp.dot(tri, chosen.astype(BF16), preferred_element_type=F32)
    cnt_t = jnp.sum(chosen, axis=0, keepdims=True)
    seg_len = jnp.floor((cnt_t + (MOE_CH - 1.0)) * (1.0 / MOE_CH)) * MOE_CH
    e_r = lax.broadcasted_iota(jnp.int32, (LANES, LANES), 0)
    e_c = lax.broadcasted_iota(jnp.int32, (LANES, LANES), 1)
    upper = jnp.where(e_r < e_c, 1.0, 0.0).astype(BF16)
    seg_off = jnp.dot(jnp.broadcast_to(seg_len, (SUBLANES, LANES)).astype(BF16), upper,
                      preferred_element_type=F32)[0:1, :]
    slot_of = before + seg_off
    rg = jnp.zeros((TM, LANES), F32)
    for k in range(TOP_K):
        slot = jnp.sum(jnp.where(lane_f == idxs[k], slot_of, 0.0), axis=1, keepdims=True)
        rg = jnp.where(lane == k, es[k] / den, rg)
        rg = jnp.where(lane == TOP_K + k, slot, rg)
    rg_ref[...] = rg
    rt_ref[...] = jnp.transpose(rg)[0:SUBLANES, :]
    row8 = lax.broadcasted_iota(jnp.int32, (SUBLANES, LANES), 0)
    carry = carry_s[...]
    meta_ref[...] = jnp.where(row8 == 0, cnt_t, jnp.where(row8 == 1, carry, 0.0)).astype(jnp.int32)
    carry = carry + jnp.floor((cnt_t + (SUBLANES - 1.0)) * (1.0 / SUBLANES)) * SUBLANES
    carry_s[...] = carry
    cnt_ref[...] = jnp.broadcast_to(carry, cnt_ref.shape).astype(jnp.int32)


def _merge(o, hf, hb, zb, x, mods, bg, nf, wvo, wol, wout, wr, br, nct, tile0):
    b, nt, _ = x.shape
    ntl = nt // TM
    tok = lambda c: pl.BlockSpec((None, TM, D), lambda bi, j: (bi, j + tile0, c))
    full = lambda *s: pl.BlockSpec(s, lambda bi, j: (0,) * len(s))
    out_nt = nt - tile0 * TM
    return pl.pallas_call(
        _merge_kernel,
        out_shape=(jax.ShapeDtypeStruct((b, out_nt, D), F32),
                   jax.ShapeDtypeStruct((b, out_nt, D), BF16),
                   jax.ShapeDtypeStruct((b, out_nt, LANES), F32),
                   jax.ShapeDtypeStruct((b, out_nt // TM, SUBLANES, TM), F32),
                   jax.ShapeDtypeStruct((b, out_nt // TM, SUBLANES, LANES), jnp.int32),
                   jax.ShapeDtypeStruct((SUBLANES, LANES), jnp.int32)),
        grid=(b, ntl - tile0),
        in_specs=[tok(0), tok(0), tok(0), tok(1), tok(2), tok(3), tok(0),
                  pl.BlockSpec((None, None, 6, 1, D),
                               lambda bi, j: (bi, _seg(j + tile0, nct), 0, 0, 0)),
                  full(2, 1, D), full(1, D), full(D, D), full(D, D), full(D, D),
                  full(D, LANES), full(1, LANES)],
        out_specs=(pl.BlockSpec((None, TM, D), lambda bi, j: (bi, j, 0)),
                   pl.BlockSpec((None, TM, D), lambda bi, j: (bi, j, 0)),
                   pl.BlockSpec((None, TM, LANES), lambda bi, j: (bi, j, 0)),
                   pl.BlockSpec((None, None, SUBLANES, TM), lambda bi, j: (bi, j, 0, 0)),
                   pl.BlockSpec((None, None, SUBLANES, LANES), lambda bi, j: (bi, j, 0, 0)),
                   pl.BlockSpec((SUBLANES, LANES), lambda bi, j: (0, 0))),
        scratch_shapes=[pltpu.VMEM((1, LANES), F32)],
        compiler_params=_cparams(("arbitrary", "arbitrary"), VMEM_LIMIT),
        name="merge_route",
    )(o, hf, hb, zb, zb, zb, x, mods, bg, nf, wvo, wol, wout, wr, br)


def _dispatch_kernel(meta_ref, h2_ref, rt_ref, xs_ref, sorted_s, zero_s, sem, *, n_blk, ntl):
    bi = pl.program_id(0)
    j = pl.program_id(1)
    first = jnp.logical_and(bi == 0, j == 0)

    @pl.when(first)
    def _():
        zero_s[...] = jnp.zeros(zero_s.shape, F32)

        def pad_copy(i):
            r0 = pl.multiple_of(i * SUBLANES, SUBLANES)
            return pltpu.make_async_copy(zero_s.at[pl.ds(0, SUBLANES), :],
                                         xs_ref.at[pl.ds(r0, SUBLANES), :], sem)

        def blk_copy(i):
            r0 = pl.multiple_of(i * MOE_R, MOE_R)
            return pltpu.make_async_copy(zero_s, xs_ref.at[pl.ds(r0, MOE_R), :], sem)

        def blk_issue(i, c):
            blk_copy(i).start()
            return c

        def blk_drain(i, c):
            blk_copy(i).wait()
            return c
        n_used = meta_ref[META_NUSED]
        lax.fori_loop(n_used, n_blk, blk_issue, 0)
        lax.fori_loop(n_used, n_blk, blk_drain, 0)

        def per_expert(e, c):
            lo = meta_ref[META_TAIL + e] // SUBLANES
            hi = meta_ref[META_END + e] // SUBLANES

            def issue(r, c2):
                pad_copy(r).start()
                return c2

            def drain(r, c2):
                pad_copy(r).wait()
                return c2
            lax.fori_loop(lo, hi, issue, 0)
            lax.fori_loop(lo, hi, drain, 0)
            return c
        lax.fori_loop(0, N_EXPERTS, per_expert, 0)

    slot_i = lax.broadcasted_iota(jnp.int32, (MOE_SLOTS, TM), 0).astype(F32)
    hit = slot_i == rt_ref[TOP_K:TOP_K + 1, :]
    for k in range(1, TOP_K):
        hit = jnp.logical_or(hit, slot_i == rt_ref[TOP_K + k:TOP_K + k + 1, :])
    onehot = jnp.where(hit, 1.0, 0.0).astype(BF16)
    tile = bi * ntl + j
    cur = tile % 2
    sorted_s[cur] = jnp.dot(onehot, h2_ref[...], preferred_element_type=F32)

    def refs_of(buf):
        return lambda slot, row: (sorted_s.at[buf, pl.ds(slot, MOE_CH), :],
                                  xs_ref.at[pl.ds(row, MOE_CH), :])

    @pl.when(tile > 0)
    def _():
        _moe_chunk_copies(meta_ref, tile - 1, sem, refs_of(1 - cur), start=False)
    _moe_chunk_copies(meta_ref, tile, sem, refs_of(cur), start=True)

    @pl.when(tile == pl.num_programs(0) * ntl - 1)
    def _():
        _moe_chunk_copies(meta_ref, tile, sem, refs_of(cur), start=False)


def _moe_chunk_copies(meta_ref, tile, sem, refs, *, start):
    base = META_TILES + tile * META_PER_TILE

    def piece(c, carry):
        src, dst = refs(pl.multiple_of(meta_ref[base + 1 + 2 * c], MOE_CH),
                        pl.multiple_of(meta_ref[base + 2 + 2 * c], SUBLANES))
        cp = pltpu.make_async_copy(src, dst, sem)
        if start:
            cp.start()
        else:
            cp.wait()
        return carry
    lax.fori_loop(0, meta_ref[base], piece, 0)


def _dispatch(meta, h2, rt, n_rows):
    b, nt, _ = h2.shape
    ntl = nt // TM
    kern = functools.partial(_dispatch_kernel, n_blk=n_rows // MOE_R, ntl=ntl)
    return pl.pallas_call(
        kern,
        out_shape=jax.ShapeDtypeStruct((n_rows, D), F32),
        grid_spec=pltpu.PrefetchScalarGridSpec(
            num_scalar_prefetch=1,
            grid=(b, ntl),
            in_specs=[pl.BlockSpec((None, TM, D), lambda bi, j, meta: (bi, j, 0)),
                      pl.BlockSpec((None, None, SUBLANES, TM), lambda bi, j, meta: (bi, j, 0, 0))],
            out_specs=pl.BlockSpec(memory_space=pl.ANY),
            scratch_shapes=[pltpu.VMEM((2, MOE_SLOTS, D), F32),
                            pltpu.VMEM((MOE_R, D), F32),
                            pltpu.SemaphoreType.DMA]),
        compiler_params=_cparams(("arbitrary", "arbitrary"), VMEM_LIMIT),
        name="moe_dispatch",
    )(meta, h2, rt)


def _expert_kernel(be_ref, nu_ref, nxt_ref, slot_ref, x_ref, wg_ref, bg_ref, wu_ref, bu_ref, wd_ref, bd_ref,
                   y_ref, wg_s, wu_s, wd_s, wf_s, sem, *, layer):
    blk = pl.program_id(0)
    used = blk < nu_ref[0]
    e = be_ref[blk]
    prev = be_ref[jnp.maximum(blk - 1, 0)]
    fresh = jnp.logical_or(blk == 0, e != prev)

    def fetch(expert, slot):
        return [pltpu.make_async_copy(w.at[layer, expert], wf_s.at[slot, m], sem.at[slot, m])
                for m, w in enumerate((wg_ref, wu_ref, wd_ref))]

    @pl.when(jnp.logical_and(used, blk == 0))
    def _():
        for cp in fetch(e, slot_ref[e]):
            cp.start()

    @pl.when(jnp.logical_and(used, fresh))
    def _():
        slot = slot_ref[e]
        for cp in fetch(e, slot):
            cp.wait()

        @pl.when(nxt_ref[e] >= 0)
        def _():
            for cp in fetch(nxt_ref[e], 1 - slot):
                cp.start()
        wg_s[...] = wf_s[slot, 0].astype(BF16)
        wu_s[...] = wf_s[slot, 1].astype(BF16)
        wd_s[...] = wf_s[slot, 2].astype(BF16)

    @pl.when(used)
    def _():
        x = x_ref[...].astype(BF16)
        g = jnp.dot(x, wg_s[...], preferred_element_type=F32) + bg_ref[...]
        u = jnp.dot(x, wu_s[...], preferred_element_type=F32) + bu_ref[...]
        g = jnp.minimum(g, SWIGLU_LIMIT)
        u = jnp.clip(u, -SWIGLU_LIMIT, SWIGLU_LIMIT)
        act = g * _sigmoid(SWIGLU_ALPHA * g) * (u + 1.0)
        y_ref[...] = jnp.dot(act.astype(BF16), wd_s[...], preferred_element_type=F32) + bd_ref[...]

    @pl.when(jnp.logical_not(used))
    def _():
        y_ref[...] = jnp.zeros(y_ref.shape, F32)


def _experts(blk_expert, n_used, nxt, slot, xs, wg, bg, wu, bu, wd, bd, layer):
    n_rows = xs.shape[0]
    n_blk = n_rows // MOE_R
    dff = wg.shape[-1]
    assert dff == D
    e_of = lambda i, be, nu: be[jnp.minimum(i, nu[0] - 1)]
    bspec = lambda c: pl.BlockSpec((None, None, 1, c),
                                   lambda i, be, nu, nx, sl: (layer, e_of(i, be, nu), 0, 0))
    hbm = pl.BlockSpec(memory_space=pl.ANY)
    return pl.pallas_call(
        functools.partial(_expert_kernel, layer=layer),
        out_shape=jax.ShapeDtypeStruct((n_rows, D), F32),
        grid_spec=pltpu.PrefetchScalarGridSpec(
            num_scalar_prefetch=4,
            grid=(n_blk,),
            in_specs=[pl.BlockSpec((MOE_R, D), lambda i, be, nu, nx, sl: (jnp.minimum(i, nu[0] - 1), 0)),
                      hbm, bspec(dff), hbm, bspec(dff), hbm, bspec(D)],
            out_specs=pl.BlockSpec((MOE_R, D), lambda i, be, nu, nx, sl: (i, 0)),
            scratch_shapes=[pltpu.VMEM((D, dff), BF16), pltpu.VMEM((D, dff), BF16),
                            pltpu.VMEM((dff, D), BF16),
                            pltpu.VMEM((2, 3, D, D), F32),
                            pltpu.SemaphoreType.DMA((2, 3))]),
        compiler_params=_cparams(("arbitrary",), VMEM_LIMIT),
        name="moe_experts",
    )(blk_expert, n_used, nxt, slot, xs, wg, bg, wu, bu, wd, bd)


def _combine_kernel(meta_ref, x_ref, rg_ref, mod_ref, fn_ref, ys_ref, o_ref, buf_s, sem, *, final, ntl):
    bi = pl.program_id(0)
    j = pl.program_id(1)

    tile = bi * ntl + j
    cur = tile % 2

    def refs_of(buf):
        return lambda slot, row: (ys_ref.at[pl.ds(row, MOE_CH), :],
                                  buf_s.at[buf, pl.ds(slot, MOE_CH), :])

    @pl.when(tile == 0)
    def _():
        buf_s[...] = jnp.zeros(buf_s.shape, F32)
        _moe_chunk_copies(meta_ref, tile, sem, refs_of(cur), start=True)

    _moe_chunk_copies(meta_ref, tile, sem, refs_of(cur), start=False)

    @pl.when(tile < pl.num_programs(0) * ntl - 1)
    def _():
        _moe_chunk_copies(meta_ref, tile + 1, sem, refs_of(1 - cur), start=True)

    rg = rg_ref[...]
    slot_i = lax.broadcasted_iota(jnp.int32, (TM, MOE_SLOTS), 1).astype(F32)
    gate = jnp.zeros((TM, MOE_SLOTS), F32)
    for k in range(TOP_K):
        gate = jnp.where(slot_i == rg[:, TOP_K + k:TOP_K + k + 1], rg[:, k:k + 1], gate)
    y = jnp.dot(gate.astype(BF16), buf_s[cur].astype(BF16), preferred_element_type=F32)
    xo = x_ref[...] + mod_ref[5] * y
    if final:
        xo = _rms(xo, fn_ref[...])
    o_ref[...] = xo


def _combine(meta, x, rg, mods, fn, ys, nct, final):
    b, nt, _ = x.shape
    ntl = nt // TM
    kern = functools.partial(_combine_kernel, final=final, ntl=ntl)
    return pl.pallas_call(
        kern,
        out_shape=jax.ShapeDtypeStruct((b, nt, D), F32),
        grid_spec=pltpu.PrefetchScalarGridSpec(
            num_scalar_prefetch=1,
            grid=(b, ntl),
            in_specs=[pl.BlockSpec((None, TM, D), lambda bi, j, meta: (bi, j, 0)),
                      pl.BlockSpec((None, TM, LANES), lambda bi, j, meta: (bi, j, 0)),
                      pl.BlockSpec((None, None, 6, 1, D),
                                   lambda bi, j, meta: (bi, _seg(j, nct), 0, 0, 0)),
                      pl.BlockSpec((1, D), lambda bi, j, meta: (0, 0)),
                      pl.BlockSpec(memory_space=pl.ANY)],
            out_specs=pl.BlockSpec((None, TM, D), lambda bi, j, meta: (bi, j, 0)),
            scratch_shapes=[pltpu.VMEM((2, MOE_SLOTS, D), F32), pltpu.SemaphoreType.DMA]),
        compiler_params=_cparams(("arbitrary", "arbitrary"), VMEM_LIMIT),
        name="moe_combine",
    )(meta, x, rg, mods, fn, ys)


_ROT_PARTNER = tuple(list(range(16, 32)) + list(range(0, 16)) + list(range(48, 64)) + list(range(32, 48)))


def _rope_tables(n_ctx, n_lat):
    n = jnp.arange(n_lat)
    row = (n // GRID_W).astype(F32)
    col = (n % GRID_W).astype(F32)
    n_freq = QK_ROPE // 4
    inv = ROPE_THETA ** (-jnp.arange(n_freq, dtype=F32) / n_freq)
    ar = row[:, None] * inv
    ac = col[:, None] * inv
    c64 = jnp.concatenate([jnp.cos(ar), jnp.cos(ar), jnp.cos(ac), jnp.cos(ac)], axis=1)
    s64 = jnp.concatenate([-jnp.sin(ar), jnp.sin(ar), -jnp.sin(ac), jnp.sin(ac)], axis=1)
    c64 = jnp.concatenate([jnp.ones((n_ctx, QK_ROPE), F32), c64], axis=0)
    s64 = jnp.concatenate([jnp.zeros((n_ctx, QK_ROPE), F32), s64], axis=0)
    nt = n_ctx + n_lat
    qs = ATTN_SCALE * LOG2E
    cq = qs * jnp.concatenate([jnp.ones((nt, QK_NOPE), F32), c64, c64], axis=1)
    sq = qs * jnp.concatenate([jnp.zeros((nt, QK_NOPE), F32), s64, s64], axis=1)
    ck = jnp.concatenate([c64, s64], axis=1)
    return cq, sq, ck


def _block_diag_pairs(w):
    w = w.reshape(LRU_BLOCKS // 2, 2, LRU_BW, LRU_BW)
    z = jnp.zeros((LRU_BLOCKS // 2, LRU_BW, LRU_BW), w.dtype)
    top = jnp.concatenate([w[:, 0], z], axis=2)
    bot = jnp.concatenate([z, w[:, 1]], axis=2)
    return jnp.concatenate([top, bot], axis=1)


def kernel(x, c, ctx, c_ctx, w_ada, b_ada, norm_mix, norm_ffn, w_in, b_branch_gate, q_norm, w_uq, kv_norm, w_ukv, w_o_attn, conv_w, conv_b, lru_w_a, lru_b_a, lru_w_x, lru_b_x, lru_lambda, w_o_lru, w_out, w_router, b_router, w_exp_gate, b_exp_gate, w_exp_up, b_exp_up, w_exp_down, b_exp_down, final_norm):
    depth = w_in.shape[0]
    b, n_lat, _ = x.shape
    n_ctx = ctx.shape[1]
    assert n_ctx % TM == 0 and n_lat % TK == 0 and b + 1 <= SUBLANES
    nct = n_ctx // TM
    nt = n_ctx + n_lat
    ntl = nt // TM
    perm = jnp.array(_ROT_PARTNER)

    xs = jnp.concatenate([ctx, x], axis=1)
    cond8 = jnp.zeros((SUBLANES, D), F32).at[:b].set(c).at[b].set(c_ctx)
    cq_t, sq_t, ck_t = _rope_tables(n_ctx, n_lat)

    for l in range(depth):
        last = l == depth - 1
        tile0 = nct if last else 0

        m8 = _ada(cond8, w_ada[l], b_ada[l][None, :]).reshape(SUBLANES, 6, 1, D)
        mods = jnp.stack([jnp.broadcast_to(m8[b], (b,) + m8.shape[1:]), m8[:b]], axis=1)

        wi = w_in[l]
        kpe_w = wi[:, Q_LORA + KV_LORA:Q_LORA + KV_LORA + QK_ROPE]
        wa = jnp.concatenate([wi[:, :Q_LORA + KV_LORA + QK_ROPE], kpe_w[:, perm]], axis=1).astype(BF16)
        wb = wi[:, Q_LORA + KV_LORA + QK_ROPE:].astype(BF16)
        uq = w_uq[l].reshape(Q_LORA, HEADS, QK_DIM).transpose(1, 0, 2)
        ukv = w_ukv[l].reshape(KV_LORA, HEADS, QK_NOPE + V_HEAD).transpose(1, 0, 2)
        wqa, wvo = _fold(uq[:, :, :QK_NOPE], ukv[:, :, :QK_NOPE], ukv[:, :, QK_NOPE:],
                         w_o_attn[l].reshape(HEADS, V_HEAD, D))
        pe = uq[:, :, QK_NOPE:]
        pe_sw = pe[:, :, perm]
        zq = jnp.zeros((HEADS, Q_LORA, QK_NOPE), F32)
        wq1 = jnp.concatenate([wqa, pe, pe], axis=2)
        wq2 = jnp.concatenate([zq, pe_sw, pe_sw], axis=2)
        wq = jnp.concatenate([wq1.transpose(1, 0, 2).reshape(Q_LORA, HEADS * QH),
                              wq2.transpose(1, 0, 2).reshape(Q_LORA, HEADS * QH)], axis=1).astype(BF16)
        wvo = wvo.reshape(D, D).astype(BF16)
        wr = jnp.zeros((D, LANES), F32).at[:, :N_EXPERTS].set(w_router[l])
        br = jnp.full((1, LANES), NEG_INF, F32).at[0, :N_EXPERTS].set(b_router[l])

        za, zb = _inproj(xs, mods, norm_mix[l][None, :], wa, wb, nct)
        q, kv, v1 = _mlaproj(za, q_norm[l][None, :], kv_norm[l][None, :], wq, cq_t, sq_t, ck_t)
        o = _attention(q, kv, v1, n_ctx)
        hs = []
        for d in range(2):
            wd = jnp.concatenate([_block_diag_pairs(lru_w_a[l, d]), _block_diag_pairs(lru_w_x[l, d])],
                                 axis=2).astype(BF16)
            hs.append(_lru(zb, conv_w[l], conv_b[l][None, :], wd, lru_b_a[l, d][None, :],
                           lru_b_x[l, d][None, :], lru_lambda[l, d][None, :], nct, reverse=d == 1))
        xn, h2, rg, rt, tile_meta, cnt = _merge(
            o, hs[0], hs[1], zb, xs, mods, b_branch_gate[l].reshape(2, 1, D), norm_ffn[l][None, :],
            wvo, w_o_lru[l].astype(BF16), w_out[l].astype(BF16), wr, br, nct, tile0)

        counts = cnt[0, :N_EXPERTS]
        padded = ((counts + (MOE_CH - 1) + MOE_R - 1) // MOE_R) * MOE_R
        pad_end = jnp.cumsum(padded)
        pad_start = pad_end - padded
        moe_nt = xn.shape[1]
        n_tiles = b * moe_nt // TM
        max_rows = (b * moe_nt * TOP_K + n_tiles * N_EXPERTS * (SUBLANES - 1)
                    + N_EXPERTS * (MOE_CH - 1))
        n_blk = max_rows // MOE_R + N_EXPERTS
        n_rows = n_blk * MOE_R
        blk_start = jnp.arange(n_blk, dtype=jnp.int32) * MOE_R
        blk_expert = jnp.minimum(jnp.sum(blk_start[:, None] >= pad_end[None, :], axis=1),
                                 N_EXPERTS - 1).astype(jnp.int32)
        n_used = (pad_end[-1:] // MOE_R).astype(jnp.int32)
        cnt_t = tile_meta[:, :, 0, :N_EXPERTS].reshape(n_tiles, N_EXPERTS)
        row0_t = pad_start[None, :] + tile_meta[:, :, 1, :N_EXPERTS].reshape(n_tiles, N_EXPERTS)
        n_ch = (cnt_t + MOE_CH - 1) // MOE_CH
        ch_end = jnp.cumsum(n_ch, axis=1)
        ch_start = ch_end - n_ch
        c_ids = jnp.arange(MOE_MAX_COPIES, dtype=jnp.int32)
        e_of_c = jnp.minimum(jnp.sum(c_ids[None, :, None] >= ch_end[:, None, :], axis=-1), N_EXPERTS - 1)
        is_e = e_of_c[:, :, None] == jnp.arange(N_EXPERTS, dtype=jnp.int32)
        pick = lambda v: jnp.sum(jnp.where(is_e, v[:, None, :], 0), axis=-1)
        first_c = pick(ch_start)
        piece = (c_ids[None, :] - first_c) * MOE_CH
        copies = jnp.stack([first_c * MOE_CH + piece, pick(row0_t) + piece], axis=-1)
        tile_tab = jnp.concatenate(
            [ch_end[:, -1:], copies.reshape(n_tiles, 2 * MOE_MAX_COPIES),
             jnp.zeros((n_tiles, META_PER_TILE - 1 - 2 * MOE_MAX_COPIES), jnp.int32)], axis=1)
        meta = jnp.concatenate([pad_start, pad_start + counts, pad_end,
                                jnp.broadcast_to(n_used, (N_EXPERTS,)),
                                tile_tab.reshape(-1)]).astype(jnp.int32)

        xsort = _dispatch(meta, h2, rt, n_rows)
        has_rows = padded > 0
        ids = jnp.arange(N_EXPERTS, dtype=jnp.int32)
        later = jnp.where(jnp.logical_and(ids[None, :] > ids[:, None], has_rows[None, :]),
                          ids[None, :], N_EXPERTS)
        nxt = jnp.min(later, axis=1)
        nxt = jnp.where(nxt == N_EXPERTS, -1, nxt).astype(jnp.int32)
        slot = ((jnp.cumsum(has_rows.astype(jnp.int32)) - 1) % 2).astype(jnp.int32)
        ys = _experts(blk_expert, n_used, nxt, slot, xsort, w_exp_gate, b_exp_gate[:, :, None, :],
                      w_exp_up, b_exp_up[:, :, None, :], w_exp_down, b_exp_down[:, :, None, :], l)
        xs = _combine(meta, xn, rg, mods, final_norm[None, :], ys, nct - tile0, last)
    return xs
```
